```python
import math
import jax, jax.numpy as jnp
from jax import lax
import numpy as np

D_MODEL = 1024
BATCH = 8
SEQ = 2048
DEPTH = 2
DEC_BATCH = 128
DEC_SEQ = 4
PAST_LEN = 2048
PAGE_SIZE = 128

N_A_LAYERS = DEPTH // 2
N_B_LAYERS = DEPTH - N_A_LAYERS
RW_HEAD = 64
RW_HEADS = D_MODEL // RW_HEAD
LORA_W = 64
LORA_A = 64
LORA_G = 128
RW_GN_EPS = 64e-5
DA_HEAD = 64
DA_HEADS = D_MODEL // (2 * DA_HEAD)
DA_VDIM = 2 * DA_HEAD
Q_BLOCK = 128
RMS_EPS = 1e-5
NEG_INF = -1e30
PEER_HEADS = 8
PEER_NKEYS = 128
PEER_NEXPERTS = PEER_NKEYS * PEER_NKEYS
PEER_QDIM = 256
PEER_HALF = PEER_QDIM // 2
PEER_TOPK = 16
PEER_BLOCK = 128
PLE_DIM = 256
DN_ALPHA = (2.0 * DEPTH) ** 0.25
DN_BETA = (8.0 * DEPTH) ** -0.25
LN_EPS = 1e-5

kernel_name = "yoco_rwkv7_diffattn_peer_step"


def layer_norm(x, g, b):
    xf = x.astype(jnp.float32)
    mu = jnp.mean(xf, -1, keepdims=True)
    var = jnp.mean(jnp.square(xf - mu), -1, keepdims=True)
    return ((xf - mu) * lax.rsqrt(var + LN_EPS) * g + b).astype(x.dtype)


def wkv7_scan(S0, r, decay, k, v, kk, a):
    def step(S, inp):
        r_t, w_t, k_t, v_t, kk_t, a_t = inp
        sa = jnp.einsum('bhvk,bhk->bhv', S, -kk_t)
        S = (S * w_t[:, :, None, :] + sa[..., None] * (kk_t * a_t)[:, :, None, :]
             + v_t[..., None] * k_t[:, :, None, :])
        return S, jnp.einsum('bhvk,bhk->bhv', S, r_t)
    xs = tuple(jnp.moveaxis(t, 1, 0) for t in (r, decay, k, v, kk, a))
    S, ys = lax.scan(step, S0, xs)
    return jnp.moveaxis(ys, 0, 1), S


def rwkv7_time_mix(x, shift_row, S0, mu, w_r, w_k, w_v, w0, w1, w2, a0, a1, a2,
                   g1, g2, k_k, k_a, r_k, gn_g, gn_b, w_o):
    f32 = jnp.float32
    B, T, D = x.shape
    x_prev = jnp.concatenate([shift_row[:, None, :].astype(x.dtype), x[:, :-1]], axis=1)
    xx = x_prev - x
    xr, xw, xk, xv, xa, xg = (x + xx * mu[i] for i in range(6))
    r = (xr @ w_r).astype(f32)
    k = (xk @ w_k).astype(f32)
    v = (xv @ w_v).astype(f32)
    w_log = -jax.nn.softplus(-(w0 + jnp.tanh(xw @ w1) @ w2).astype(f32)) - 0.5
    decay = jnp.exp(-jnp.exp(w_log))
    a = jax.nn.sigmoid((a0 + (xa @ a1) @ a2).astype(f32))
    g = jax.nn.sigmoid(xg @ g1) @ g2
    heads = lambda t: t.reshape(B, T, RW_HEADS, RW_HEAD)
    kk = heads(k * k_k)
    kk = kk / jnp.maximum(jnp.sqrt(jnp.sum(kk * kk, -1, keepdims=True)), 1e-12)
    k = k * (1.0 + (a - 1.0) * k_a)
    rh, kh, vh = heads(r), heads(k), heads(v)
    y, S = wkv7_scan(S0.astype(f32), rh, heads(decay), kh, vh, kk, heads(a))
    mu_y = jnp.mean(y, -1, keepdims=True)
    var_y = jnp.mean(jnp.square(y - mu_y), -1, keepdims=True)
    yn = ((y - mu_y) * lax.rsqrt(var_y + RW_GN_EPS)).reshape(B, T, D) * gn_g + gn_b
    bonus = (jnp.sum(rh * kh * r_k, -1, keepdims=True) * vh).reshape(B, T, D)
    out = ((yn + bonus).astype(x.dtype) * g) @ w_o
    return out, x[:, -1], S.astype(S0.dtype)


def alibi_slopes():
    return 2.0 ** (-8.0 * jnp.arange(1, DA_HEADS + 1, dtype=jnp.float32) / DA_HEADS)


def diff_softmax_core(q, k, v, q_pos, k_pos, lam):
    s = jnp.einsum('bqhcd,bkhcd->bhcqk', q, k).astype(jnp.float32) * (DA_HEAD ** -0.5)
    dist = (q_pos[:, None] - k_pos[None, :]).astype(jnp.float32)
    s = s - alibi_slopes()[None, :, None, None, None] * dist
    s = jnp.where(dist >= 0, s, NEG_INF)
    p = jax.nn.softmax(s, axis=-1)
    att = p[:, :, 0] - lam * p[:, :, 1]
    return jnp.einsum('bhqk,bkhv->bqhv', att.astype(v.dtype), v)


def diff_attn_mixer(x, k_all, v_all, past_len, w_q, lam_vec, subln_g, w_o, lam_init):
    f32 = jnp.float32
    B, T, D = x.shape
    q = (x @ w_q).reshape(B, T, DA_HEADS, 2, DA_HEAD)
    lv = lam_vec.astype(f32)
    lam = jnp.exp(jnp.sum(lv[0] * lv[1])) - jnp.exp(jnp.sum(lv[2] * lv[3])) + lam_init
    Tk = k_all.shape[1]
    k_pos = jnp.arange(Tk)
    kh = k_all.reshape(B, Tk, DA_HEADS, 2, DA_HEAD)
    q_pos = past_len + jnp.arange(T)
    if T % Q_BLOCK == 0:
        nb = T // Q_BLOCK
        qb = jnp.moveaxis(q.reshape(B, nb, Q_BLOCK, DA_HEADS, 2, DA_HEAD), 1, 0)
        pb = q_pos.reshape(nb, Q_BLOCK)
        o = lax.map(lambda args: diff_softmax_core(args[0], kh, v_all, args[1], k_pos, lam), (qb, pb))
        o = jnp.moveaxis(o, 0, 1).reshape(B, T, DA_HEADS, DA_VDIM)
    else:
        o = diff_softmax_core(q, kh, v_all, q_pos, k_pos, lam)
    of = o.astype(f32)
    of = of * lax.rsqrt(jnp.mean(of * of, -1, keepdims=True) + RMS_EPS) * subln_g
    o = (of * (1.0 - lam_init)).astype(x.dtype).reshape(B, T, D)
    return o @ w_o


def peer_ffn(x, w_q, b_q, subkeys, u_tab, v_tab):
    B, T, D = x.shape
    n = B * T
    n_pad = (-n) % PEER_BLOCK
    xb = jnp.pad(x.reshape(n, D), ((0, n_pad), (0, 0))).reshape(-1, PEER_BLOCK, D)

    def block(xblk):
        q = (xblk @ w_q + b_q).reshape(PEER_BLOCK, PEER_HEADS, 2, PEER_HALF)
        sc = jnp.einsum('thcd,cnd->thcn', q, subkeys).astype(jnp.float32)
        s_half, i_half = lax.top_k(sc, PEER_TOPK)
        cand = (s_half[:, :, 0, :, None] + s_half[:, :, 1, None, :]).reshape(
            PEER_BLOCK, PEER_HEADS, PEER_TOPK * PEER_TOPK)
        s_sel, c_sel = lax.top_k(cand, PEER_TOPK)
        i1 = jnp.take_along_axis(i_half[:, :, 0], c_sel // PEER_TOPK, axis=-1)
        i2 = jnp.take_along_axis(i_half[:, :, 1], c_sel % PEER_TOPK, axis=-1)
        expert = i1 * PEER_NKEYS + i2
        gate = jax.nn.softmax(s_sel, axis=-1)
        h = jnp.einsum('td,thkd->thk', xblk, u_tab[expert])
        act = (gate * jax.nn.gelu(h.astype(jnp.float32))).astype(xblk.dtype)
        return jnp.einsum('thk,thkd->td', act, v_tab[expert])

    out = lax.map(block, xb).reshape(-1, D)[:n]
    return out.reshape(B, T, D)


def per_layer_embed(x, p_i, w_p, w_g, b_g):
    return x + (p_i @ w_p) * jax.nn.sigmoid(x @ w_g + b_g)


def lambda_init(layer):
    return 0.8 - 0.6 * math.exp(-0.3 * layer)


def setup_inputs(seed: int = 0) -> dict:
    key = jax.random.key(seed)
    keys = jax.random.split(key, 64)
    counter = [0]

    def nk():
        counter[0] += 1
        return keys[counter[0] - 1]

    f32 = jnp.float32
    D = D_MODEL

    def nrm(shape, scale):
        return jax.random.normal(nk(), shape, f32) * scale

    def unif(shape, lo, hi):
        return jax.random.uniform(nk(), shape, f32, lo, hi)

    n_pages = PAST_LEN // PAGE_SIZE
    n_used = DEC_BATCH * n_pages
    n_pool = n_used + max(1, n_used // 4)
    NA, NB = N_A_LAYERS, N_B_LAYERS
    inp = {}
    inp['x_prompt'] = nrm((BATCH, SEQ, D), 1.0)
    inp['x_sample'] = nrm((DEC_BATCH, DEC_SEQ, D), 1.0)
    inp['state_shift'] = nrm((NA, DEC_BATCH, D), 1.0)
    inp['state_wkv'] = nrm((NA, DEC_BATCH, RW_HEADS, RW_HEAD, RW_HEAD), 0.5)
    inp['cache_k'] = nrm((n_pool, PAGE_SIZE, DA_HEADS, 2 * DA_HEAD), 1.0)
    inp['cache_v'] = nrm((n_pool, PAGE_SIZE, DA_HEADS, DA_VDIM), 1.0)
    inp['page_table'] = jax.random.permutation(nk(), n_pool)[:n_used].reshape(
        DEC_BATCH, n_pages).astype(jnp.int32)
    inp['p_prompt'] = nrm((DEPTH, BATCH, SEQ, PLE_DIM), 1.0)
    inp['p_sample'] = nrm((DEPTH, DEC_BATCH, DEC_SEQ, PLE_DIM), 1.0)
    inp['rw_mu'] = unif((NA, 6, D), 0.0, 1.0)
    inp['rw_w_r'] = nrm((NA, D, D), D ** -0.5)
    inp['rw_w_k'] = nrm((NA, D, D), D ** -0.5)
    inp['rw_w_v'] = nrm((NA, D, D), D ** -0.5)
    inp['rw_w0'] = unif((NA, D), -6.0, -0.5)
    inp['rw_w1'] = nrm((NA, D, LORA_W), D ** -0.5)
    inp['rw_w2'] = nrm((NA, LORA_W, D), 0.1 * LORA_W ** -0.5)
    inp['rw_a0'] = nrm((NA, D), 0.1)
    inp['rw_a1'] = nrm((NA, D, LORA_A), D ** -0.5)
    inp['rw_a2'] = nrm((NA, LORA_A, D), 0.1 * LORA_A ** -0.5)
    inp['rw_g1'] = nrm((NA, D, LORA_G), D ** -0.5)
    inp['rw_g2'] = nrm((NA, LORA_G, D), LORA_G ** -0.5)
    inp['rw_k_k'] = 0.85 + nrm((NA, D), 0.05)
    inp['rw_k_a'] = 1.0 + nrm((NA, D), 0.05)
    inp['rw_r_k'] = nrm((NA, RW_HEADS, RW_HEAD), 0.1)
    inp['rw_gn_g'] = 1.0 + nrm((NA, D), 0.02)
    inp['rw_gn_b'] = nrm((NA, D), 0.02)
    inp['rw_w_o'] = nrm((NA, D, D), DN_BETA * D ** -0.5)
    inp['da_w_k'] = nrm((D, DA_HEADS * 2 * DA_HEAD), D ** -0.5)
    inp['da_w_v'] = nrm((D, DA_HEADS * DA_VDIM), D ** -0.5)
    inp['da_w_q'] = nrm((NB, D, DA_HEADS * 2 * DA_HEAD), D ** -0.5)
    inp['da_lam'] = nrm((NB, 4, DA_HEAD), 0.1)
    inp['da_subln_g'] = 1.0 + nrm((NB, DA_VDIM), 0.02)
    inp['da_w_o'] = nrm((NB, DA_HEADS * DA_VDIM, D), DN_BETA * D ** -0.5)
    inp['ln1_g'] = 1.0 + nrm((DEPTH, D), 0.02)
    inp['ln1_b'] = nrm((DEPTH, D), 0.02)
    inp['ln2_g'] = 1.0 + nrm((DEPTH, D), 0.02)
    inp['ln2_b'] = nrm((DEPTH, D), 0.02)
    inp['peer_w_q'] = nrm((DEPTH, D, PEER_HEADS * PEER_QDIM), D ** -0.5)
    inp['peer_b_q'] = nrm((DEPTH, PEER_HEADS * PEER_QDIM), 0.02)
    inp['peer_subkeys'] = nrm((DEPTH, 2, PEER_NKEYS, PEER_HALF), PEER_HALF ** -0.5)
    inp['peer_u'] = nrm((DEPTH, PEER_NEXPERTS, D), D ** -0.5)
    inp['peer_v'] = nrm((DEPTH, PEER_NEXPERTS, D), DN_BETA * PEER_HEADS ** -0.5)
    inp['ple_w_p'] = nrm((DEPTH, PLE_DIM, D), DN_BETA * PLE_DIM ** -0.5)
    inp['ple_w_g'] = nrm((DEPTH, D, D), D ** -0.5)
    inp['ple_b_g'] = nrm((DEPTH, D), 0.02)
    return inp


def reference(x_prompt, x_sample, state_shift, state_wkv, cache_k, cache_v, page_table,
              p_prompt, p_sample,
              rw_mu, rw_w_r, rw_w_k, rw_w_v, rw_w0, rw_w1, rw_w2, rw_a0, rw_a1, rw_a2,
              rw_g1, rw_g2, rw_k_k, rw_k_a, rw_r_k, rw_gn_g, rw_gn_b, rw_w_o,
              da_w_k, da_w_v, da_w_q, da_lam, da_subln_g, da_w_o,
              ln1_g, ln1_b, ln2_g, ln2_b,
              peer_w_q, peer_b_q, peer_subkeys, peer_u, peer_v,
              ple_w_p, ple_w_g, ple_b_g):

    def layer_stack(x, pemb, shift0, wkv0, past_k, past_v):
        B, T, D = x.shape
        past_len = 0 if past_k is None else past_k.shape[1]
        shifts, wkvs = [], []
        k_sh = v_sh = k_all = v_all = None
        for layer in range(DEPTH):
            if layer < N_A_LAYERS:
                i = layer
                h, last_row, S = rwkv7_time_mix(
                    x, shift0[i], wkv0[i], rw_mu[i], rw_w_r[i], rw_w_k[i], rw_w_v[i],
                    rw_w0[i], rw_w1[i], rw_w2[i], rw_a0[i], rw_a1[i], rw_a2[i],
                    rw_g1[i], rw_g2[i], rw_k_k[i], rw_k_a[i], rw_r_k[i],
                    rw_gn_g[i], rw_gn_b[i], rw_w_o[i])
                shifts.append(last_row)
                wkvs.append(S)
            else:
                j = layer - N_A_LAYERS
                h = diff_attn_mixer(x, k_all, v_all, past_len, da_w_q[j], da_lam[j],
                                    da_subln_g[j], da_w_o[j], lambda_init(layer))
            x = layer_norm(DN_ALPHA * x + h, ln1_g[layer], ln1_b[layer])
            c = peer_ffn(x, peer_w_q[layer], peer_b_q[layer], peer_subkeys[layer],
                         peer_u[layer], peer_v[layer])
            x = layer_norm(DN_ALPHA * x + c, ln2_g[layer], ln2_b[layer])
            x = per_layer_embed(x, pemb[layer], ple_w_p[layer], ple_w_g[layer], ple_b_g[layer])
            if layer == N_A_LAYERS - 1:
                k_sh = (x @ da_w_k).reshape(B, T, DA_HEADS, 2 * DA_HEAD)
                v_sh = (x @ da_w_v).reshape(B, T, DA_HEADS, DA_VDIM)
                if past_k is None:
                    k_all, v_all = k_sh, v_sh
                else:
                    k_all = jnp.concatenate([past_k.astype(k_sh.dtype), k_sh], axis=1)
                    v_all = jnp.concatenate([past_v.astype(v_sh.dtype), v_sh], axis=1)
        return x, jnp.stack(shifts), jnp.stack(wkvs), k_sh, v_sh

    Bp = x_prompt.shape[0]
    shift0_p = jnp.zeros((N_A_LAYERS, Bp, D_MODEL), x_prompt.dtype)
    wkv0_p = jnp.zeros((N_A_LAYERS, Bp, RW_HEADS, RW_HEAD, RW_HEAD), state_wkv.dtype)
    y_prompt, shift_p, wkv_p, k_p, v_p = layer_stack(x_prompt, p_prompt, shift0_p, wkv0_p, None, None)

    Bs = x_sample.shape[0]
    n_pages = page_table.shape[1]
    past_len = n_pages * cache_k.shape[1]
    past_k = cache_k[page_table].reshape(Bs, past_len, DA_HEADS, 2 * DA_HEAD)
    past_v = cache_v[page_table].reshape(Bs, past_len, DA_HEADS, DA_VDIM)
    y_sample, shift_s, wkv_s, k_s, v_s = layer_stack(x_sample, p_sample, state_shift, state_wkv,
                                                     past_k, past_v)
    return (y_prompt, y_sample, shift_p, wkv_p, k_p, v_p, shift_s, wkv_s, k_s, v_s)
```

```python
import functools
import math

import jax
import jax.numpy as jnp
from jax import lax
from jax.experimental import pallas as pl
from jax.experimental.pallas import tpu as pltpu

F32 = jnp.float32
BF16 = jnp.bfloat16

D_MODEL = 1024
DEPTH = 2
RW_HEAD = 64
RW_HEADS = D_MODEL // RW_HEAD
RW_GN_EPS = 64e-5
DA_HEAD = 64
DA_HEADS = D_MODEL // (2 * DA_HEAD)
DA_VDIM = 2 * DA_HEAD
RMS_EPS = 1e-5
NEG_INF = -1e30
PEER_HEADS = 8
PEER_NKEYS = 128
PEER_NEXPERTS = PEER_NKEYS * PEER_NKEYS
PEER_QDIM = 256
PEER_HALF = PEER_QDIM // 2
PEER_TOPK = 16
DN_ALPHA = (2.0 * DEPTH) ** 0.25
LN_EPS = 1e-5

LANES = 128
VMEM_LIMIT = 56 * 1024 * 1024

ROW_TILE = 256
ROUTE_TILE = 256
PEER_TOK_TILE = 512
PEER_EXP_TILE = 512
SCAN_TCHUNK = 32
SCAN_VROWS = 4
ATT_TILE = 512


def _cparams(n_axes):
    return pltpu.CompilerParams(dimension_semantics=("arbitrary",) * n_axes,
                                vmem_limit_bytes=VMEM_LIMIT)


def _dot(a, b):
    return jnp.dot(a, b, preferred_element_type=F32)


def _dot_nt(a, b):
    return lax.dot_general(a, b, (((1,), (1,)), ((), ())), preferred_element_type=F32)


def _layer_norm(x, g, b):
    mu = jnp.mean(x, axis=-1, keepdims=True)
    xc = x - mu
    var = jnp.mean(xc * xc, axis=-1, keepdims=True)
    return xc * lax.rsqrt(var + LN_EPS) * g + b


def _split2(x):
    hi = x.astype(BF16)
    lo = (x - hi.astype(F32)).astype(BF16)
    return hi, lo


def _head_sum(x, ind, ind_t):
    hi, lo = _split2(x)
    s = _dot(hi, ind) + _dot(lo, ind)
    shi, slo = _split2(s)
    return _dot(shi, ind_t) + _dot(slo, ind_t)


def _full(shape):
    n = len(shape)
    return pl.BlockSpec(shape, lambda *_: (0,) * n)


def _rwkv_pre_kernel(x_ref, xp_ref, mu_ref, wr_ref, wk_ref, wv_ref, w1_ref, w2_ref, a1_ref, a2_ref,
                     g1_ref, g2_ref, vec_ref, ind_ref, indt_ref,
                     r_ref, w_ref, k_ref, v_ref, a_ref, b_ref, g_ref):
    x = x_ref[...]
    xx = xp_ref[...] - x
    mix = lambda i: (x + xx * mu_ref[i:i + 1, :]).astype(BF16)
    xr, xw, xk, xv, xa, xg = (mix(i) for i in range(6))
    w0, a0, k_k, k_a = (vec_ref[i:i + 1, :] for i in range(4))
    r = _dot(xr, wr_ref[...])
    k = _dot(xk, wk_ref[...])
    v = _dot(xv, wv_ref[...])
    wl = w0 + _dot(jnp.tanh(_dot(xw, w1_ref[...])).astype(BF16), w2_ref[...])
    z = -wl
    softplus = jnp.maximum(z, 0.0) + jnp.log(1.0 + jnp.exp(-jnp.abs(z)))
    decay = jnp.exp(-jnp.exp(-softplus - 0.5))
    a = jax.nn.sigmoid(a0 + _dot(_dot(xa, a1_ref[...]).astype(BF16), a2_ref[...]))
    g = _dot(jax.nn.sigmoid(_dot(xg, g1_ref[...])).astype(BF16), g2_ref[...])
    kk = k * k_k
    n2 = _head_sum(kk * kk, ind_ref[...], indt_ref[...])
    kk = kk / jnp.maximum(jnp.sqrt(n2), 1e-12)
    r_ref[...] = r
    w_ref[...] = decay
    k_ref[...] = k * (1.0 + (a - 1.0) * k_a)
    v_ref[...] = v
    a_ref[...] = -kk
    b_ref[...] = kk * a
    g_ref[...] = g


def _rwkv_pre(x, xp, mu, w_r, w_k, w_v, w1, w2, a1, a2, g1, g2, vec, ind, ind_t):
    n = x.shape[0]
    tm = ROW_TILE
    row = pl.BlockSpec((tm, D_MODEL), lambda i: (i, 0))
    consts = (mu, w_r, w_k, w_v, w1, w2, a1, a2, g1, g2, vec, ind, ind_t)
    return pl.pallas_call(
        _rwkv_pre_kernel,
        grid=(n // tm,),
        in_specs=[row, row] + [_full(c.shape) for c in consts],
        out_specs=[row] * 7,
        out_shape=[jax.ShapeDtypeStruct((n, D_MODEL), F32)] * 7,
        compiler_params=_cparams(1),
    )(x, xp, *consts)


def _scan_kernel(r_ref, w_ref, k_ref, v_ref, a_ref, b_ref, s0_ref, y_ref, st_ref, *, tchunk):
    @pl.when(pl.program_id(1) == 0)
    def _():
        st_ref[...] = s0_ref[...]

    def vloop(vc, carry):
        v0 = vc * SCAN_VROWS
        state = tuple(st_ref[0, v0 + j] for j in range(SCAN_VROWS))

        def step(t, st):
            a = a_ref[0, t]
            w = w_ref[0, t]
            b = b_ref[0, t]
            k = k_ref[0, t]
            r = r_ref[0, t]
            out = []
            for j in range(SCAN_VROWS):
                s = st[j]
                val = v_ref[0, t, pl.ds(v0 + j, 1), :]
                sa = jnp.sum(s * a, axis=0, keepdims=True)
                s = s * w + sa * b + val * k
                y_ref[0, t, pl.ds(v0 + j, 1), :] = jnp.sum(s * r, axis=0, keepdims=True)
                out.append(s)
            return tuple(out)

        state = lax.fori_loop(0, tchunk, step, state)
        for j in range(SCAN_VROWS):
            st_ref[0, v0 + j] = state[j]
        return carry

    lax.fori_loop(0, RW_HEAD // SCAN_VROWS, vloop, 0)


def _wkv_scan(r, w, k, v, a, b, s0):
    g, t = r.shape[0], r.shape[1]
    tchunk = min(SCAN_TCHUNK, t)
    seq = pl.BlockSpec((1, tchunk, RW_HEAD, LANES), lambda gi, ti: (gi, ti, 0, 0))
    st = pl.BlockSpec((1, RW_HEAD, RW_HEAD, LANES), lambda gi, ti: (gi, 0, 0, 0))
    return pl.pallas_call(
        functools.partial(_scan_kernel, tchunk=tchunk),
        grid=(g, t // tchunk),
        in_specs=[seq] * 6 + [st],
        out_specs=[seq, st],
        out_shape=[jax.ShapeDtypeStruct(r.shape, F32), jax.ShapeDtypeStruct(s0.shape, F32)],
        compiler_params=_cparams(2),
    )(r, w, k, v, a, b, s0)


def _rwkv_post_kernel(x_ref, y_ref, r_ref, k_ref, v_ref, g_ref, vec_ref, ind_ref, indt_ref, wo_ref,
                      o_ref):
    ind, ind_t = ind_ref[...], indt_ref[...]
    gn_g, gn_b, r_k, ln_g, ln_b = (vec_ref[i:i + 1, :] for i in range(5))
    y = y_ref[...]
    inv = 1.0 / RW_HEAD
    yc = y - _head_sum(y, ind, ind_t) * inv
    var = _head_sum(yc * yc, ind, ind_t) * inv
    yn = yc * lax.rsqrt(var + RW_GN_EPS) * gn_g + gn_b
    v = v_ref[...]
    bonus = _head_sum(r_ref[...] * k_ref[...] * r_k, ind, ind_t) * v
    h = _dot(((yn + bonus) * g_ref[...]).astype(BF16), wo_ref[...])
    o_ref[...] = _layer_norm(DN_ALPHA * x_ref[...] + h, ln_g, ln_b)


def _rwkv_post(x, y, r, k, v, g, vec, ind, ind_t, w_o):
    n = x.shape[0]
    tm = ROW_TILE
    row = pl.BlockSpec((tm, D_MODEL), lambda i: (i, 0))
    consts = (vec, ind, ind_t, w_o)
    return pl.pallas_call(
        _rwkv_post_kernel,
        grid=(n // tm,),
        in_specs=[row] * 6 + [_full(c.shape) for c in consts],
        out_specs=row,
        out_shape=jax.ShapeDtypeStruct((n, D_MODEL), F32),
        compiler_params=_cparams(1),
    )(x, y, r, k, v, g, *consts)


def _proj_ln_kernel(x_ref, h_ref, vec_ref, wo_ref, o_ref):
    h = _dot(h_ref[...].astype(BF16), wo_ref[...])
    o_ref[...] = _layer_norm(DN_ALPHA * x_ref[...] + h, vec_ref[0:1, :], vec_ref[1:2, :])


def _proj_ln(x, h, vec, w_o):
    n = x.shape[0]
    tm = ROW_TILE
    row = pl.BlockSpec((tm, D_MODEL), lambda i: (i, 0))
    return pl.pallas_call(
        _proj_ln_kernel,
        grid=(n // tm,),
        in_specs=[row, row, _full(vec.shape), _full(w_o.shape)],
        out_specs=row,
        out_shape=jax.ShapeDtypeStruct((n, D_MODEL), F32),
        compiler_params=_cparams(1),
    )(x, h, vec, w_o)


def _peer_route_kernel(x_ref, wq_ref, bq_ref, sk_ref, xt_ref, s2_ref, e2_ref, e1_ref, cut_ref,
                       top_ref, cnt_ref):
    xt = x_ref[...].T.astype(BF16)
    xt_ref[...] = xt
    q = _dot(wq_ref[...], xt) + bq_ref[...]
    for h in range(PEER_HEADS):
        sc = []
        for c in range(2):
            lo = (2 * h + c) * PEER_HALF
            s = _dot(sk_ref[c], q[lo:lo + PEER_HALF, :].astype(BF16))
            sc.append(s)
            vals = s
            for j in range(PEER_TOPK):
                m = jnp.max(vals, axis=0, keepdims=True)
                hit = vals >= m
                top_ref[c, j:j + 1, :] = m
                cnt_ref[c, j:j + 1, :] = jnp.sum(jnp.where(hit, 1.0, 0.0), axis=0, keepdims=True)
                vals = jnp.where(hit, -jnp.inf, vals)
        ta, tb = top_ref[0], top_ref[1]
        ca, cb = cnt_ref[0], cnt_ref[1]
        half = PEER_TOPK // 2
        cand = [ta[0:1, :] + tb[:half], ta[0:1, :] + tb[half:]]
        mult = [ca[0:1, :] * cb[:half], ca[0:1, :] * cb[half:]]
        for i in range(1, PEER_TOPK):
            cand.append(ta[i:i + 1, :] + tb[:half])
            mult.append(ca[i:i + 1, :] * cb[:half])
        best = thr = zsum = None
        remaining = jnp.full((1, xt.shape[1]), float(PEER_TOPK), F32)
        for j in range(PEER_TOPK):
            m = jnp.max(functools.reduce(jnp.maximum, cand), axis=0, keepdims=True)
            hits = [cj >= m for cj in cand]
            n_hit = functools.reduce(jnp.add, [jnp.where(hj, mj, 0.0) for hj, mj in zip(hits, mult)])
            take = jnp.minimum(jnp.sum(n_hit, axis=0, keepdims=True), remaining)
            if j == 0:
                best, thr, zsum = m, m, take
            else:
                zsum = zsum + take * jnp.exp(m - best)
                thr = jnp.where(take > 0.0, m, thr)
            remaining = remaining - take
            cand = [jnp.where(hj, -jnp.inf, cj) for hj, cj in zip(hits, cand)]
        s1, s2 = sc
        cut = jnp.full_like(s1, jnp.inf)
        for j in range(PEER_TOPK):
            bj = tb[j:j + 1, :]
            cut = jnp.where(s1 + bj >= thr, bj, cut)
        s2_ref[h] = s2
        cut_ref[h] = cut
        e1_ref[h] = jnp.exp(s1 - ta[0:1, :])
        e2_ref[h] = jnp.exp(s2 - tb[0:1, :]) / zsum


def _peer_route(x, wq_t, bq_b, subkeys):
    n = x.shape[0]
    tb = ROUTE_TILE
    tab = pl.BlockSpec((PEER_HEADS, PEER_NKEYS, tb), lambda i: (0, 0, i))
    tab_shape = jax.ShapeDtypeStruct((PEER_HEADS, PEER_NKEYS, n), F32)
    return pl.pallas_call(
        _peer_route_kernel,
        grid=(n // tb,),
        in_specs=[pl.BlockSpec((tb, D_MODEL), lambda i: (i, 0)),
                  _full(wq_t.shape), _full(bq_b.shape), _full(subkeys.shape)],
        out_specs=[pl.BlockSpec((D_MODEL, tb), lambda i: (0, i)), tab, tab, tab, tab],
        out_shape=[jax.ShapeDtypeStruct((D_MODEL, n), BF16)] + [tab_shape] * 4,
        scratch_shapes=[pltpu.VMEM((2, PEER_TOPK, tb), F32), pltpu.VMEM((2, PEER_TOPK, tb), F32)],
        compiler_params=_cparams(1),
    )(x, wq_t, bq_b, subkeys)


def _peer_sweep_kernel(x_ref, xt_ref, s2_ref, e2_ref, e1_ref, cut_ref, u_ref, vt_ref, vec_ref,
                       o_ref, acc_ref, act_ref):
    e = pl.program_id(1)

    @pl.when(e == 0)
    def _():
        acc_ref[...] = jnp.zeros_like(acc_ref)

    ht = _dot(u_ref[...], xt_ref[...])
    n_sub = PEER_EXP_TILE // PEER_NKEYS
    for j in range(n_sub):
        i1 = e * n_sub + j
        gate = None
        for h in range(PEER_HEADS):
            e1 = e1_ref[h, pl.ds(i1, 1), :]
            cut = cut_ref[h, pl.ds(i1, 1), :]
            term = e1 * jnp.where(s2_ref[h] >= cut, e2_ref[h], 0.0)
            gate = term if gate is None else gate + term
        hj = ht[j * PEER_NKEYS:(j + 1) * PEER_NKEYS, :]
        act_ref[j * PEER_NKEYS:(j + 1) * PEER_NKEYS, :] = (
            gate * jax.nn.gelu(hj, approximate=True)).astype(BF16)
    acc_ref[...] += _dot(vt_ref[...], act_ref[...])

    @pl.when(e == pl.num_programs(1) - 1)
    def _():
        c = acc_ref[...].T
        o_ref[...] = _layer_norm(DN_ALPHA * x_ref[...] + c, vec_ref[0:1, :], vec_ref[1:2, :])


def _peer_sweep(x, xt, s2, e2, e1, cut, u_bf, vt_bf, vec):
    n = x.shape[0]
    tb, eb = PEER_TOK_TILE, PEER_EXP_TILE
    tab = pl.BlockSpec((PEER_HEADS, PEER_NKEYS, tb), lambda i, e: (0, 0, i))
    return pl.pallas_call(
        _peer_sweep_kernel,
        grid=(n // tb, PEER_NEXPERTS // eb),
        in_specs=[pl.BlockSpec((tb, D_MODEL), lambda i, e: (i, 0)),
                  pl.BlockSpec((D_MODEL, tb), lambda i, e: (0, i)),
                  tab, tab, tab, tab,
                  pl.BlockSpec((eb, D_MODEL), lambda i, e: (e, 0)),
                  pl.BlockSpec((D_MODEL, eb), lambda i, e: (0, e)),
                  pl.BlockSpec(vec.shape, lambda i, e: (0, 0))],
        out_specs=pl.BlockSpec((tb, D_MODEL), lambda i, e: (i, 0)),
        out_shape=jax.ShapeDtypeStruct((n, D_MODEL), F32),
        scratch_shapes=[pltpu.VMEM((D_MODEL, tb), F32), pltpu.VMEM((eb, tb), BF16)],
        compiler_params=_cparams(2),
    )(x, xt, s2, e2, e1, cut, u_bf, vt_bf, vec)


def _ple_kernel(x_ref, p_ref, wp_ref, wg_ref, bg_ref, *rest, with_extra):
    x = x_ref[...]
    gate = jax.nn.sigmoid(_dot(x.astype(BF16), wg_ref[...]) + bg_ref[...])
    x3 = x + _dot(p_ref[...].astype(BF16), wp_ref[...]) * gate
    if with_extra:
        we_ref, o_ref, e_ref = rest
        e_ref[...] = _dot(x3.astype(BF16), we_ref[...])
    else:
        (o_ref,) = rest
    o_ref[...] = x3


def _ple(x, p, w_p, w_g, b_g, w_extra=None):
    n = x.shape[0]
    tm = ROW_TILE
    row = pl.BlockSpec((tm, D_MODEL), lambda i: (i, 0))
    in_specs = [row, pl.BlockSpec((tm, p.shape[1]), lambda i: (i, 0)),
                _full(w_p.shape), _full(w_g.shape), _full(b_g.shape)]
    args = [x, p, w_p, w_g, b_g]
    out_specs = [row]
    out_shape = [jax.ShapeDtypeStruct((n, D_MODEL), F32)]
    if w_extra is not None:
        ne = w_extra.shape[1]
        in_specs.append(_full(w_extra.shape))
        args.append(w_extra)
        out_specs.append(pl.BlockSpec((tm, ne), lambda i: (i, 0)))
        out_shape.append(jax.ShapeDtypeStruct((n, ne), F32))
    out = pl.pallas_call(
        functools.partial(_ple_kernel, with_extra=w_extra is not None),
        grid=(n // tm,),
        in_specs=in_specs, out_specs=out_specs, out_shape=out_shape,
        compiler_params=_cparams(1),
    )(*args)
    return out if w_extra is not None else (out[0], None)


def _subln(of, g, lam_init):
    of = of * lax.rsqrt(jnp.mean(of * of, axis=-1, keepdims=True) + RMS_EPS) * g
    return of * (1.0 - lam_init)


def _attn_prompt_kernel(scal_ref, q_ref, k_ref, v_ref, g_ref, o_ref, kb_ref, vb_ref,
                        m_ref, l_ref, acc_ref, *, lam_init):
    h = pl.program_id(1)
    qi = pl.program_id(2)
    tq = ATT_TILE
    slope = scal_ref[h]
    lam = scal_ref[DA_HEADS]

    @pl.when(qi == 0)
    def _():
        kb_ref[...] = k_ref[...].astype(BF16)
        vb_ref[...] = v_ref[...].astype(BF16)

    q = q_ref[...] * (DA_HEAD ** -0.5)
    lane = lax.broadcasted_iota(jnp.int32, q.shape, 1)
    qs = jnp.concatenate([jnp.where(lane < DA_HEAD, q, 0.0), jnp.where(lane >= DA_HEAD, q, 0.0)],
                         axis=0).astype(BF16)
    d0 = (lax.broadcasted_iota(jnp.int32, (2 * tq, tq), 0) % tq
          - lax.broadcasted_iota(jnp.int32, (2 * tq, tq), 1))
    bias0 = slope * d0.astype(F32)
    m_ref[...] = jnp.full_like(m_ref, -jnp.inf)
    l_ref[...] = jnp.zeros_like(l_ref)
    acc_ref[...] = jnp.zeros_like(acc_ref)

    def block(ki, bias):
        off = pl.multiple_of(ki * tq, tq)
        s = _dot_nt(qs, kb_ref[pl.ds(off, tq), :]) - bias
        m_prev = m_ref[...]
        m_new = jnp.maximum(m_prev, jnp.max(s, axis=-1, keepdims=True))
        alpha = jnp.exp(m_prev - m_new)
        p = jnp.exp(s - m_new)
        l_ref[...] = alpha * l_ref[...] + jnp.sum(p, axis=-1, keepdims=True)
        acc_ref[...] = alpha * acc_ref[...] + _dot(p.astype(BF16), vb_ref[pl.ds(off, tq), :])
        m_ref[...] = m_new

    def past(ki, carry):
        block(ki, bias0 + slope * ((qi - ki) * tq).astype(F32))
        return carry

    lax.fori_loop(0, qi, past, 0)
    block(qi, jnp.where(d0 >= 0, bias0, -NEG_INF))
    o = acc_ref[...] / l_ref[...]
    of = o[:tq] - lam * o[tq:]
    o_ref[...] = _subln(of, g_ref[...], lam_init)


def _attn_prompt(scal, q, k, v, g, batch, seq, lam_init):
    tq = ATT_TILE
    nq = seq // tq
    grid_spec = pltpu.PrefetchScalarGridSpec(
        num_scalar_prefetch=1,
        grid=(batch, DA_HEADS, nq),
        in_specs=[pl.BlockSpec((tq, DA_VDIM), lambda b, h, i, s: (b * nq + i, h)),
                  pl.BlockSpec((seq, DA_VDIM), lambda b, h, i, s: (b, h)),
                  pl.BlockSpec((seq, DA_VDIM), lambda b, h, i, s: (b, h)),
                  pl.BlockSpec((1, DA_VDIM), lambda b, h, i, s: (0, 0))],
        out_specs=pl.BlockSpec((tq, DA_VDIM), lambda b, h, i, s: (b * nq + i, h)),
        scratch_shapes=[pltpu.VMEM((seq, DA_VDIM), BF16), pltpu.VMEM((seq, DA_VDIM), BF16),
                        pltpu.VMEM((2 * tq, 1), F32), pltpu.VMEM((2 * tq, 1), F32),
                        pltpu.VMEM((2 * tq, DA_VDIM), F32)])
    return pl.pallas_call(
        functools.partial(_attn_prompt_kernel, lam_init=lam_init),
        grid_spec=grid_spec,
        out_shape=jax.ShapeDtypeStruct((batch * seq, D_MODEL), F32),
        compiler_params=_cparams(3),
    )(scal, q, k, v, g)


SAMPLE_ROWS = 2 * 4 * DA_HEADS


def _attn_sample_kernel(pt_ref, scal_ref, q_ref, k_ref, v_ref, kn_ref, vn_ref, g_ref, o_ref,
                        qbd_ref, kp_ref, vp_ref, m_ref, l_ref, acc_ref, *, dec_seq, page, lam_init):
    p = pl.program_id(1)
    n_pages = pl.num_programs(1)
    past_len = n_pages * page
    rows = SAMPLE_ROWS
    row = lax.broadcasted_iota(jnp.int32, (rows, 1), 0)
    row_q = (row // DA_HEADS) % dec_seq
    row_h = row % DA_HEADS
    slope = jnp.zeros((rows, 1), F32)
    for h in range(DA_HEADS):
        slope = jnp.where(row_h == h, scal_ref[h], slope)

    @pl.when(p == 0)
    def _():
        q4 = q_ref[0] * (DA_HEAD ** -0.5)
        rep = jnp.concatenate([jnp.broadcast_to(q4[i:i + 1, :], (DA_HEADS, D_MODEL))
                               for i in range(dec_seq)] * 2, axis=0)
        r = lax.broadcasted_iota(jnp.int32, (rows, D_MODEL), 0)
        ln = lax.broadcasted_iota(jnp.int32, (rows, D_MODEL), 1)
        keep = (ln // DA_VDIM == r % DA_HEADS) & ((ln // DA_HEAD) % 2 == r // (rows // 2))
        qbd_ref[...] = jnp.where(keep, rep, 0.0).astype(BF16)
        m_ref[...] = jnp.full_like(m_ref, -jnp.inf)
        l_ref[...] = jnp.zeros_like(l_ref)
        acc_ref[...] = jnp.zeros_like(acc_ref)
        kp_ref[...] = jnp.zeros_like(kp_ref)
        vp_ref[...] = jnp.zeros_like(vp_ref)
        kp_ref[0:8, :] = kn_ref[0].astype(BF16)
        vp_ref[0:8, :] = vn_ref[0].astype(BF16)

    def block(kb, vb, dist, visible):
        s = _dot_nt(qbd_ref[...], kb) - slope * dist
        if visible is not None:
            s = jnp.where(visible, s, NEG_INF)
        m_prev = m_ref[...]
        m_new = jnp.maximum(m_prev, jnp.max(s, axis=-1, keepdims=True))
        alpha = jnp.exp(m_prev - m_new)
        pr = jnp.exp(s - m_new)
        l_ref[...] = alpha * l_ref[...] + jnp.sum(pr, axis=-1, keepdims=True)
        acc_ref[...] = alpha * acc_ref[...] + _dot(pr.astype(BF16), vb)
        m_ref[...] = m_new

    tok = lax.broadcasted_iota(jnp.int32, (rows, page), 1)
    dist = (past_len + row_q - p * page - tok).astype(F32)
    block(k_ref[0].astype(BF16), v_ref[0].astype(BF16), dist, None)

    @pl.when(p == n_pages - 1)
    def _():
        dn = row_q - tok
        block(kp_ref[...], vp_ref[...], dn.astype(F32), dn >= 0)
        o = acc_ref[...] / l_ref[...]
        lam = scal_ref[DA_HEADS]
        diff = o[:rows // 2] - lam * o[rows // 2:]
        hh = lax.broadcasted_iota(jnp.int32, (DA_HEADS, D_MODEL), 0)
        ln = lax.broadcasted_iota(jnp.int32, (DA_HEADS, D_MODEL), 1)
        own = ln // DA_VDIM == hh
        for i in range(dec_seq):
            of = jnp.where(own, diff[i * DA_HEADS:(i + 1) * DA_HEADS, :], 0.0)
            ms = jnp.sum(of * of, axis=-1, keepdims=True) * (1.0 / DA_VDIM)
            of = of * lax.rsqrt(ms + RMS_EPS) * g_ref[...] * (1.0 - lam_init)
            o_ref[0, i:i + 1, :] = jnp.sum(of, axis=0, keepdims=True)


def _attn_sample(page_table, scal, q, cache_k, cache_v, k_new, v_new, g_tiled, lam_init):
    b, dec_seq, _ = q.shape
    n_pages = page_table.shape[1]
    page = cache_k.shape[1]
    assert dec_seq * DA_HEADS * 2 == SAMPLE_ROWS and dec_seq <= 8
    grid_spec = pltpu.PrefetchScalarGridSpec(
        num_scalar_prefetch=2,
        grid=(b, n_pages),
        in_specs=[pl.BlockSpec((1, dec_seq, D_MODEL), lambda i, p, pt, s: (i, 0, 0)),
                  pl.BlockSpec((1, page, D_MODEL), lambda i, p, pt, s: (pt[i * n_pages + p], 0, 0)),
                  pl.BlockSpec((1, page, D_MODEL), lambda i, p, pt, s: (pt[i * n_pages + p], 0, 0)),
                  pl.BlockSpec((1, 8, D_MODEL), lambda i, p, pt, s: (i, 0, 0)),
                  pl.BlockSpec((1, 8, D_MODEL), lambda i, p, pt, s: (i, 0, 0)),
                  pl.BlockSpec((1, D_MODEL), lambda i, p, pt, s: (0, 0))],
        out_specs=pl.BlockSpec((1, dec_seq, D_MODEL), lambda i, p, pt, s: (i, 0, 0)),
        scratch_shapes=[pltpu.VMEM((SAMPLE_ROWS, D_MODEL), BF16),
                        pltpu.VMEM((page, D_MODEL), BF16), pltpu.VMEM((page, D_MODEL), BF16),
                        pltpu.VMEM((SAMPLE_ROWS, 1), F32), pltpu.VMEM((SAMPLE_ROWS, 1), F32),
                        pltpu.VMEM((SAMPLE_ROWS, D_MODEL), F32)])
    return pl.pallas_call(
        functools.partial(_attn_sample_kernel, dec_seq=dec_seq, page=page, lam_init=lam_init),
        grid_spec=grid_spec,
        out_shape=jax.ShapeDtypeStruct((b, dec_seq, D_MODEL), F32),
        compiler_params=_cparams(2),
    )(page_table.reshape(-1), scal, q, cache_k, cache_v, k_new, v_new, g_tiled)


def _lambda_init(layer):
    return 0.8 - 0.6 * math.exp(-0.3 * layer)


def _peer_layer(x, layer, peer_w_q, peer_b_q, peer_subkeys, peer_u, peer_v, ln2_g, ln2_b):
    wq_t = peer_w_q[layer].T.astype(BF16)
    bq_b = jnp.broadcast_to(peer_b_q[layer][:, None], (PEER_HEADS * PEER_QDIM, ROUTE_TILE))
    xt, s2, e2, e1, cut = _peer_route(x, wq_t, bq_b, peer_subkeys[layer].astype(BF16))
    vec = jnp.stack([ln2_g[layer], ln2_b[layer]])
    return _peer_sweep(x, xt, s2, e2, e1, cut, peer_u[layer].astype(BF16),
                       peer_v[layer].T.astype(BF16), vec)


def kernel(x_prompt, x_sample, state_shift, state_wkv, cache_k, cache_v, page_table, p_prompt, p_sample, rw_mu, rw_w_r, rw_w_k, rw_w_v, rw_w0, rw_w1, rw_w2, rw_a0, rw_a1, rw_a2, rw_g1, rw_g2, rw_k_k, rw_k_a, rw_r_k, rw_gn_g, rw_gn_b, rw_w_o, da_w_k, da_w_v, da_w_q, da_lam, da_subln_g, da_w_o, ln1_g, ln1_b, ln2_g, ln2_b, peer_w_q, peer_b_q, peer_subkeys, peer_u, peer_v, ple_w_p, ple_w_g, ple_b_g):
    bp, tp, d = x_prompt.shape
    bs, ts, _ = x_sample.shape
    assert d == D_MODEL and bp * RW_HEADS == LANES and bs == LANES
    np_, ns = bp * tp, bs * ts
    hh, hd = RW_HEADS, RW_HEAD
    bf = lambda w: w.astype(BF16)

    x = jnp.concatenate([x_prompt.reshape(np_, d), x_sample.reshape(ns, d)], axis=0)
    n = np_ + ns
    assert n % PEER_TOK_TILE == 0

    xp_prev = jnp.concatenate([jnp.zeros((bp, 1, d), F32), x_prompt[:, :-1]], axis=1)
    xs_prev = jnp.concatenate([state_shift[0][:, None, :], x_sample[:, :-1]], axis=1)
    x_prev = jnp.concatenate([xp_prev.reshape(np_, d), xs_prev.reshape(ns, d)], axis=0)
    head_of_lane = jnp.arange(d) // hd
    ind = (head_of_lane[:, None] == jnp.arange(LANES)[None, :]).astype(BF16)
    ind_t = ind.T
    vec_pre = jnp.stack([rw_w0[0], rw_a0[0], rw_k_k[0], rw_k_a[0]])
    r, w, k, v, a, b, g = _rwkv_pre(
        x, x_prev, rw_mu[0], bf(rw_w_r[0]), bf(rw_w_k[0]), bf(rw_w_v[0]), bf(rw_w1[0]), bf(rw_w2[0]),
        bf(rw_a1[0]), bf(rw_a2[0]), bf(rw_g1[0]), bf(rw_g2[0]), vec_pre, ind, ind_t)

    to_p = lambda t: t[:np_].reshape(bp, tp, hh, hd).transpose(1, 3, 0, 2).reshape(1, tp, hd, LANES)
    to_s = lambda t: t[np_:].reshape(bs, ts, hh, hd).transpose(2, 1, 3, 0)
    y_p, st_p = _wkv_scan(*(to_p(t) for t in (r, w, k, v, a, b)),
                          jnp.zeros((1, hd, hd, LANES), F32))
    s0_s = state_wkv[0].astype(F32).transpose(1, 2, 3, 0)
    y_s, st_s = _wkv_scan(*(to_s(t) for t in (r, w, k, v, a, b)), s0_s)
    y = jnp.concatenate([
        y_p.reshape(tp, hd, bp, hh).transpose(2, 0, 3, 1).reshape(np_, d),
        y_s.transpose(3, 1, 0, 2).reshape(ns, d)], axis=0)
    wkv_p = st_p.reshape(hd, hd, bp, hh).transpose(2, 3, 0, 1)[None].astype(state_wkv.dtype)
    wkv_s = st_s.transpose(3, 0, 1, 2)[None].astype(state_wkv.dtype)
    shift_p = x_prompt[:, -1][None]
    shift_s = x_sample[:, -1][None]

    vec_post = jnp.stack([rw_gn_g[0], rw_gn_b[0], rw_r_k[0].reshape(d), ln1_g[0], ln1_b[0]])
    x = _rwkv_post(x, y, r, k, v, g, vec_post, ind, ind_t, bf(rw_w_o[0]))
    x = _peer_layer(x, 0, peer_w_q, peer_b_q, peer_subkeys, peer_u, peer_v, ln2_g, ln2_b)
    w_extra = bf(jnp.concatenate([da_w_k, da_w_v, da_w_q[0]], axis=1))
    p_all = jnp.concatenate([p_prompt.reshape(DEPTH, np_, -1), p_sample.reshape(DEPTH, ns, -1)], axis=1)
    x, kvq = _ple(x, p_all[0], bf(ple_w_p[0]), bf(ple_w_g[0]), ple_b_g[0][None], w_extra)
    k_sh, v_sh, q = kvq[:, :d], kvq[:, d:2 * d], kvq[:, 2 * d:]

    lam_init = _lambda_init(1)
    lv = da_lam[0].astype(F32)
    lam = jnp.exp(jnp.sum(lv[0] * lv[1])) - jnp.exp(jnp.sum(lv[2] * lv[3])) + lam_init
    slopes = 2.0 ** (-8.0 * jnp.arange(1, DA_HEADS + 1, dtype=F32) / DA_HEADS)
    scal = jnp.concatenate([slopes, lam[None]]).astype(F32)
    o_p = _attn_prompt(scal, q[:np_], k_sh[:np_], v_sh[:np_], da_subln_g[0][None], bp, tp, lam_init)
    pad = lambda t: jnp.pad(t[np_:].reshape(bs, ts, d), ((0, 0), (0, 8 - ts), (0, 0)))
    pool, page = cache_k.shape[0], cache_k.shape[1]
    o_s = _attn_sample(page_table, scal, q[np_:].reshape(bs, ts, d),
                       cache_k.reshape(pool, page, d), cache_v.reshape(pool, page, d),
                       pad(k_sh), pad(v_sh), jnp.tile(da_subln_g[0], DA_HEADS)[None], lam_init)
    o = jnp.concatenate([o_p, o_s.reshape(ns, d)], axis=0)
    x = _proj_ln(x, o, jnp.stack([ln1_g[1], ln1_b[1]]), bf(da_w_o[0]))
    x = _peer_layer(x, 1, peer_w_q, peer_b_q, peer_subkeys, peer_u, peer_v, ln2_g, ln2_b)
    x, _ = _ple(x, p_all[1], bf(ple_w_p[1]), bf(ple_w_g[1]), ple_b_g[1][None])

    kv_shape_p = (bp, tp, DA_HEADS, DA_VDIM)
    kv_shape_s = (bs, ts, DA_HEADS, DA_VDIM)
    return (x[:np_].reshape(bp, tp, d), x[np_:].reshape(bs, ts, d),
            shift_p, wkv_p,
            k_sh[:np_].reshape(kv_shape_p), v_sh[:np_].reshape(kv_shape_p),
            shift_s, wkv_s,
            k_sh[np_:].reshape(kv_shape_s), v_sh[np_:].reshape(kv_shape_s))
```

```python
import functools
import math

import jax
import jax.numpy as jnp
from jax import lax
from jax.experimental import pallas as pl
from jax.experimental.pallas import tpu as pltpu

F32 = jnp.float32
BF16 = jnp.bfloat16

D_MODEL = 1024
DEPTH = 2
RW_HEAD = 64
RW_HEADS = D_MODEL // RW_HEAD
RW_GN_EPS = 64e-5
DA_HEAD = 64
DA_HEADS = D_MODEL // (2 * DA_HEAD)
DA_VDIM = 2 * DA_HEAD
RMS_EPS = 1e-5
NEG_INF = -1e30
PEER_HEADS = 8
PEER_NKEYS = 128
PEER_NEXPERTS = PEER_NKEYS * PEER_NKEYS
PEER_QDIM = 256
PEER_HALF = PEER_QDIM // 2
PEER_TOPK = 16
DN_ALPHA = (2.0 * DEPTH) ** 0.25
LN_EPS = 1e-5

LANES = 128
BF16_ROWS = 16
VMEM_LIMIT = 56 * 1024 * 1024

ROW_TILE = 256
ROUTE_TILE = 256
PEER_TOK_TILE = 512
PEER_EXP_TILE = 1024
SCAN_TCHUNK = 32
SCAN_VROWS = 4
ATT_TILE = 256
SAMPLE_PAGES = 4


def _cparams(n_axes):
    return pltpu.CompilerParams(dimension_semantics=("arbitrary",) * n_axes,
                                vmem_limit_bytes=VMEM_LIMIT)


def _dot(a, b):
    return jnp.dot(a, b, preferred_element_type=F32)


def _dot_nt(a, b):
    return lax.dot_general(a, b, (((1,), (1,)), ((), ())), preferred_element_type=F32)


def _layer_norm(x, g, b):
    mu = jnp.mean(x, axis=-1, keepdims=True)
    xc = x - mu
    var = jnp.mean(xc * xc, axis=-1, keepdims=True)
    return xc * lax.rsqrt(var + LN_EPS) * g + b


def _split2(x):
    hi = x.astype(BF16)
    lo = (x - hi.astype(F32)).astype(BF16)
    return hi, lo


def _head_sum(x, ind, ind_t):
    hi, lo = _split2(x)
    s = _dot(hi, ind) + _dot(lo, ind)
    shi, slo = _split2(s)
    return _dot(shi, ind_t) + _dot(slo, ind_t)


def _full(shape):
    n = len(shape)
    return pl.BlockSpec(shape, lambda *_: (0,) * n)


def _rwkv_pre_kernel(x_ref, xp_ref, mu_ref, wr_ref, wk_ref, wv_ref, w1_ref, w2_ref, a1_ref, a2_ref,
                     g1_ref, g2_ref, vec_ref, ind_ref, indt_ref,
                     r_ref, w_ref, k_ref, v_ref, a_ref, b_ref, g_ref):
    x = x_ref[...]
    xx = xp_ref[...] - x
    mix = lambda i: (x + xx * mu_ref[i:i + 1, :]).astype(BF16)
    xr, xw, xk, xv, xa, xg = (mix(i) for i in range(6))
    w0, a0, k_k, k_a = (vec_ref[i:i + 1, :] for i in range(4))
    r = _dot(xr, wr_ref[...])
    k = _dot(xk, wk_ref[...])
    v = _dot(xv, wv_ref[...])
    wl = w0 + _dot(jnp.tanh(_dot(xw, w1_ref[...])).astype(BF16), w2_ref[...])
    z = -wl
    softplus = jnp.maximum(z, 0.0) + jnp.log(1.0 + jnp.exp(-jnp.abs(z)))
    decay = jnp.exp(-jnp.exp(-softplus - 0.5))
    a = jax.nn.sigmoid(a0 + _dot(_dot(xa, a1_ref[...]).astype(BF16), a2_ref[...]))
    g = _dot(jax.nn.sigmoid(_dot(xg, g1_ref[...])).astype(BF16), g2_ref[...])
    kk = k * k_k
    n2 = _head_sum(kk * kk, ind_ref[...], indt_ref[...])
    kk = kk / jnp.maximum(jnp.sqrt(n2), 1e-12)
    r_ref[...] = r
    w_ref[...] = decay
    k_ref[...] = k * (1.0 + (a - 1.0) * k_a)
    v_ref[...] = v
    a_ref[...] = -kk
    b_ref[...] = kk * a
    g_ref[...] = g


def _rwkv_pre(x, xp, mu, w_r, w_k, w_v, w1, w2, a1, a2, g1, g2, vec, ind, ind_t):
    n = x.shape[0]
    tm = ROW_TILE
    row = pl.BlockSpec((tm, D_MODEL), lambda i: (i, 0))
    consts = (mu, w_r, w_k, w_v, w1, w2, a1, a2, g1, g2, vec, ind, ind_t)
    return pl.pallas_call(
        _rwkv_pre_kernel,
        grid=(n // tm,),
        in_specs=[row, row] + [_full(c.shape) for c in consts],
        out_specs=[row] * 7,
        out_shape=[jax.ShapeDtypeStruct((n, D_MODEL), F32)] * 7,
        compiler_params=_cparams(1),
    )(x, xp, *consts)


def _scan_kernel(r_ref, w_ref, k_ref, v_ref, a_ref, b_ref, s0_ref, y_ref, st_ref, wr_ref, dot_ref,
                 *, tchunk):
    @pl.when(pl.program_id(1) == 0)
    def _():
        st_ref[...] = s0_ref[...]

    def prep(t, carry):
        r = r_ref[0, t]
        wr_ref[t] = w_ref[0, t] * r
        dot_ref[t, 0:1, :] = jnp.sum(b_ref[0, t] * r, axis=0, keepdims=True)
        dot_ref[t, 1:2, :] = jnp.sum(k_ref[0, t] * r, axis=0, keepdims=True)
        return carry

    lax.fori_loop(0, tchunk, prep, 0)

    def vloop(vc, carry):
        v0 = vc * SCAN_VROWS
        state = tuple(st_ref[0, v0 + j] for j in range(SCAN_VROWS))

        def step(t, st):
            a = a_ref[0, t]
            wr = wr_ref[t]
            w = w_ref[0, t]
            b = b_ref[0, t]
            k = k_ref[0, t]
            br = dot_ref[t, 0:1, :]
            kr = dot_ref[t, 1:2, :]
            out = []
            for j in range(SCAN_VROWS):
                s = st[j]
                val = v_ref[0, t, pl.ds(v0 + j, 1), :]
                sa = jnp.sum(s * a, axis=0, keepdims=True)
                y0 = jnp.sum(s * wr, axis=0, keepdims=True)
                y_ref[0, t, pl.ds(v0 + j, 1), :] = y0 + sa * br + val * kr
                out.append(s * w + sa * b + val * k)
            return tuple(out)

        state = lax.fori_loop(0, tchunk, step, state)
        for j in range(SCAN_VROWS):
            st_ref[0, v0 + j] = state[j]
        return carry

    lax.fori_loop(0, RW_HEAD // SCAN_VROWS, vloop, 0)


def _wkv_scan(r, w, k, v, a, b, s0):
    g, t = r.shape[0], r.shape[1]
    tchunk = min(SCAN_TCHUNK, t)
    seq = pl.BlockSpec((1, tchunk, RW_HEAD, LANES), lambda gi, ti: (gi, ti, 0, 0))
    st = pl.BlockSpec((1, RW_HEAD, RW_HEAD, LANES), lambda gi, ti: (gi, 0, 0, 0))
    return pl.pallas_call(
        functools.partial(_scan_kernel, tchunk=tchunk),
        grid=(g, t // tchunk),
        in_specs=[seq] * 6 + [st],
        out_specs=[seq, st],
        out_shape=[jax.ShapeDtypeStruct(r.shape, F32), jax.ShapeDtypeStruct(s0.shape, F32)],
        scratch_shapes=[pltpu.VMEM((tchunk, RW_HEAD, LANES), F32), pltpu.VMEM((tchunk, 8, LANES), F32)],
        compiler_params=_cparams(2),
    )(r, w, k, v, a, b, s0)


def _rwkv_post_kernel(x_ref, y_ref, r_ref, k_ref, v_ref, g_ref, vec_ref, ind_ref, indt_ref, wo_ref,
                      o_ref):
    ind, ind_t = ind_ref[...], indt_ref[...]
    gn_g, gn_b, r_k, ln_g, ln_b = (vec_ref[i:i + 1, :] for i in range(5))
    y = y_ref[...]
    inv = 1.0 / RW_HEAD
    yc = y - _head_sum(y, ind, ind_t) * inv
    var = _head_sum(yc * yc, ind, ind_t) * inv
    yn = yc * lax.rsqrt(var + RW_GN_EPS) * gn_g + gn_b
    v = v_ref[...]
    bonus = _head_sum(r_ref[...] * k_ref[...] * r_k, ind, ind_t) * v
    h = _dot(((yn + bonus) * g_ref[...]).astype(BF16), wo_ref[...])
    o_ref[...] = _layer_norm(DN_ALPHA * x_ref[...] + h, ln_g, ln_b)


def _rwkv_post(x, y, r, k, v, g, vec, ind, ind_t, w_o):
    n = x.shape[0]
    tm = ROW_TILE
    row = pl.BlockSpec((tm, D_MODEL), lambda i: (i, 0))
    consts = (vec, ind, ind_t, w_o)
    return pl.pallas_call(
        _rwkv_post_kernel,
        grid=(n // tm,),
        in_specs=[row] * 6 + [_full(c.shape) for c in consts],
        out_specs=row,
        out_shape=jax.ShapeDtypeStruct((n, D_MODEL), F32),
        compiler_params=_cparams(1),
    )(x, y, r, k, v, g, *consts)


def _proj_ln_kernel(x_ref, h_ref, vec_ref, wo_ref, o_ref):
    h = _dot(h_ref[...].astype(BF16), wo_ref[...])
    o_ref[...] = _layer_norm(DN_ALPHA * x_ref[...] + h, vec_ref[0:1, :], vec_ref[1:2, :])


def _proj_ln(x, h, vec, w_o):
    n = x.shape[0]
    tm = ROW_TILE
    row = pl.BlockSpec((tm, D_MODEL), lambda i: (i, 0))
    return pl.pallas_call(
        _proj_ln_kernel,
        grid=(n // tm,),
        in_specs=[row, row, _full(vec.shape), _full(w_o.shape)],
        out_specs=row,
        out_shape=jax.ShapeDtypeStruct((n, D_MODEL), F32),
        compiler_params=_cparams(1),
    )(x, h, vec, w_o)


NO_RANK = 127.0


def _bf16_pair_words(x):
    bits = lax.bitcast_convert_type(x.astype(BF16).astype(F32), jnp.int32)
    return bits | lax.shift_right_logical(bits, jnp.full_like(bits, 16))


def _peer_route_kernel(x_ref, wq_ref, bq_ref, sk_ref, xt_ref, rk_ref, e2_ref, e1_ref, ns_ref,
                       top_ref, cnt_ref):
    xt = x_ref[...].T.astype(BF16)
    xt_ref[...] = xt
    q = _dot(wq_ref[...], xt) + bq_ref[...]
    for h in range(PEER_HEADS):
        sc = []
        rank = None
        for c in range(2):
            lo = (2 * h + c) * PEER_HALF
            s = _dot(sk_ref[c], q[lo:lo + PEER_HALF, :].astype(BF16))
            sc.append(s)
            vals = s
            rank = jnp.full_like(s, NO_RANK)
            for j in range(PEER_TOPK):
                m = jnp.max(vals, axis=0, keepdims=True)
                hit = vals >= m
                top_ref[c, j:j + 1, :] = m
                cnt_ref[c, j:j + 1, :] = jnp.sum(jnp.where(hit, 1.0, 0.0), axis=0, keepdims=True)
                vals = jnp.where(hit, -jnp.inf, vals)
                if c == 1:
                    rank = jnp.where(hit, float(j), rank)
        ta, tb = top_ref[0], top_ref[1]
        ca, cb = cnt_ref[0], cnt_ref[1]
        half = PEER_TOPK // 2
        cand = [ta[0:1, :] + tb[:half], ta[0:1, :] + tb[half:]]
        mult = [ca[0:1, :] * cb[:half], ca[0:1, :] * cb[half:]]
        for i in range(1, PEER_TOPK):
            cand.append(ta[i:i + 1, :] + tb[:half])
            mult.append(ca[i:i + 1, :] * cb[:half])
        best = thr = zsum = None
        remaining = jnp.full((1, xt.shape[1]), float(PEER_TOPK), F32)
        for j in range(PEER_TOPK):
            m = jnp.max(functools.reduce(jnp.maximum, cand), axis=0, keepdims=True)
            hits = [cj >= m for cj in cand]
            n_hit = functools.reduce(jnp.add, [jnp.where(hj, mj, 0.0) for hj, mj in zip(hits, mult)])
            take = jnp.minimum(jnp.sum(n_hit, axis=0, keepdims=True), remaining)
            if j == 0:
                best, thr, zsum = m, m, take
            else:
                zsum = zsum + take * jnp.exp(m - best)
                thr = jnp.where(take > 0.0, m, thr)
            remaining = remaining - take
            cand = [jnp.where(hj, -jnp.inf, cj) for hj, cj in zip(hits, cand)]
        s1, s2 = sc
        nsel = jnp.zeros_like(s1)
        for j in range(PEER_TOPK):
            nsel = nsel + jnp.where(s1 + tb[j:j + 1, :] >= thr, 1.0, 0.0)
        rk_ref[h] = rank.astype(BF16)
        ns_ref[h] = _bf16_pair_words(nsel)
        e1_ref[h] = _bf16_pair_words(jnp.exp(s1 - ta[0:1, :]))
        e2_ref[h] = (jnp.exp(s2 - tb[0:1, :]) / zsum).astype(BF16)


def _peer_route(x, wq_t, bq_b, subkeys):
    n = x.shape[0]
    tb = ROUTE_TILE
    tab = pl.BlockSpec((PEER_HEADS, PEER_NKEYS, tb), lambda i: (0, 0, i))
    tab_words = jax.ShapeDtypeStruct((PEER_HEADS, PEER_NKEYS, n), jnp.int32)
    tab_bf16 = jax.ShapeDtypeStruct((PEER_HEADS, PEER_NKEYS, n), BF16)
    return pl.pallas_call(
        _peer_route_kernel,
        grid=(n // tb,),
        in_specs=[pl.BlockSpec((tb, D_MODEL), lambda i: (i, 0)),
                  _full(wq_t.shape), _full(bq_b.shape), _full(subkeys.shape)],
        out_specs=[pl.BlockSpec((D_MODEL, tb), lambda i: (0, i)), tab, tab, tab, tab],
        out_shape=[jax.ShapeDtypeStruct((D_MODEL, n), BF16), tab_bf16, tab_bf16, tab_words, tab_words],
        scratch_shapes=[pltpu.VMEM((2, PEER_TOPK, tb), F32), pltpu.VMEM((2, PEER_TOPK, tb), F32)],
        compiler_params=_cparams(1),
    )(x, wq_t, bq_b, subkeys)


def _gelu_tanh(x):
    c0 = math.sqrt(2.0 / math.pi)
    inner = x * (jnp.asarray(c0, x.dtype) + jnp.asarray(c0 * 0.044715, x.dtype) * (x * x))
    hx = jnp.asarray(0.5, x.dtype) * x
    return hx + hx * jnp.tanh(inner)


def _peer_sweep_kernel(x_ref, xt_ref, rk_ref, e2_ref, e1_ref, ns_ref, u_ref, vt_ref, vec_ref,
                       o_ref, acc_ref, act_ref):
    e = pl.program_id(1)

    @pl.when(e == 0)
    def _():
        acc_ref[...] = jnp.zeros_like(acc_ref)

    n_sub = PEER_EXP_TILE // PEER_NKEYS
    tb = xt_ref.shape[1]
    zero = jnp.zeros((), BF16)
    first = pl.multiple_of(e * n_sub, n_sub)

    def row_tile(ref, h, j):
        row = ref[h, pl.ds(first, n_sub), :][j:j + 1, :]
        tile = pltpu.bitcast(jnp.broadcast_to(row, (BF16_ROWS // 2, tb)), BF16)
        return jnp.concatenate([tile] * (PEER_NKEYS // BF16_ROWS), axis=0)

    for j in range(n_sub):
        rows = slice(j * PEER_NKEYS, (j + 1) * PEER_NKEYS)
        gate = None
        for h in range(PEER_HEADS):
            keep = rk_ref[h] < row_tile(ns_ref, h, j)
            term = jnp.where(keep, e2_ref[h], zero) * row_tile(e1_ref, h, j)
            gate = term if gate is None else gate + term
        ht = _dot(u_ref[rows, :], xt_ref[...])
        act_ref[rows, :] = gate * _gelu_tanh(ht.astype(BF16))
    acc_ref[...] += _dot(vt_ref[...], act_ref[...])

    @pl.when(e == pl.num_programs(1) - 1)
    def _():
        c = acc_ref[...].T
        o_ref[...] = _layer_norm(DN_ALPHA * x_ref[...] + c, vec_ref[0:1, :], vec_ref[1:2, :])


def _peer_sweep(x, xt, rk, e2, e1, nsel, u_bf, vt_bf, vec):
    n = x.shape[0]
    tb, eb = PEER_TOK_TILE, PEER_EXP_TILE
    tab = pl.BlockSpec((PEER_HEADS, PEER_NKEYS, tb), lambda i, e: (0, 0, i))
    return pl.pallas_call(
        _peer_sweep_kernel,
        grid=(n // tb, PEER_NEXPERTS // eb),
        in_specs=[pl.BlockSpec((tb, D_MODEL), lambda i, e: (i, 0)),
                  pl.BlockSpec((D_MODEL, tb), lambda i, e: (0, i)),
                  tab, tab, tab, tab,
                  pl.BlockSpec((eb, D_MODEL), lambda i, e: (e, 0)),
                  pl.BlockSpec((D_MODEL, eb), lambda i, e: (0, e)),
                  pl.BlockSpec(vec.shape, lambda i, e: (0, 0))],
        out_specs=pl.BlockSpec((tb, D_MODEL), lambda i, e: (i, 0)),
        out_shape=jax.ShapeDtypeStruct((n, D_MODEL), F32),
        scratch_shapes=[pltpu.VMEM((D_MODEL, tb), F32), pltpu.VMEM((eb, tb), BF16)],
        compiler_params=_cparams(2),
    )(x, xt, rk, e2, e1, nsel, u_bf, vt_bf, vec)


def _ple_kernel(x_ref, p_ref, wp_ref, wg_ref, bg_ref, *rest, n_extra):
    we_refs, o_ref, e_refs = rest[:n_extra], rest[n_extra], rest[n_extra + 1:]
    x = x_ref[...]
    gate = jax.nn.sigmoid(_dot(x.astype(BF16), wg_ref[...]) + bg_ref[...])
    x3 = x + _dot(p_ref[...].astype(BF16), wp_ref[...]) * gate
    o_ref[...] = x3
    x3b = x3.astype(BF16)
    for we_ref, e_ref in zip(we_refs, e_refs):
        e_ref[...] = _dot(x3b, we_ref[...])


def _ple(x, p, w_p, w_g, b_g, w_extra=()):
    n = x.shape[0]
    tm = ROW_TILE
    row = pl.BlockSpec((tm, D_MODEL), lambda i: (i, 0))
    n_extra = len(w_extra)
    return pl.pallas_call(
        functools.partial(_ple_kernel, n_extra=n_extra),
        grid=(n // tm,),
        in_specs=[row, pl.BlockSpec((tm, p.shape[1]), lambda i: (i, 0)),
                  _full(w_p.shape), _full(w_g.shape), _full(b_g.shape)]
                 + [_full(w.shape) for w in w_extra],
        out_specs=[row] * (1 + n_extra),
        out_shape=[jax.ShapeDtypeStruct((n, D_MODEL), F32)] * (1 + n_extra),
        compiler_params=_cparams(1),
    )(x, p, w_p, w_g, b_g, *w_extra)


def _subln(of, g, lam_init):
    of = of * lax.rsqrt(jnp.mean(of * of, axis=-1, keepdims=True) + RMS_EPS) * g
    return of * (1.0 - lam_init)


def _attn_prompt_kernel(scal_ref, q_ref, k_ref, v_ref, g_ref, o_ref, kb_ref, vt_ref, bias_ref,
                        m_ref, l_ref, acc_ref, *, lam_init):
    h = pl.program_id(1)
    qi = pl.program_id(2)
    tq = ATT_TILE
    slope = scal_ref[h]
    lam = scal_ref[DA_HEADS]
    d0 = (lax.broadcasted_iota(jnp.int32, (tq, 2 * tq), 1) % tq
          - lax.broadcasted_iota(jnp.int32, (tq, 2 * tq), 0))

    @pl.when(qi == 0)
    def _():
        kb_ref[...] = k_ref[...].astype(BF16)
        vt_ref[...] = v_ref[...].T.astype(BF16)
        bias_ref[...] = slope * d0.astype(F32)

    qt = (q_ref[...] * (DA_HEAD ** -0.5)).T
    sub = lax.broadcasted_iota(jnp.int32, qt.shape, 0)
    qs = jnp.concatenate([jnp.where(sub < DA_HEAD, qt, 0.0), jnp.where(sub >= DA_HEAD, qt, 0.0)],
                         axis=1).astype(BF16)
    m_ref[...] = jnp.full_like(m_ref, -jnp.inf)
    l_ref[...] = jnp.zeros_like(l_ref)
    acc_ref[...] = jnp.zeros_like(acc_ref)

    def block(ki, bias, shift):
        off = pl.multiple_of(ki * tq, tq)
        s = _dot(kb_ref[pl.ds(off, tq), :], qs) - bias
        m_prev = m_ref[...]
        m_new = jnp.maximum(m_prev, jnp.max(s, axis=0, keepdims=True) - shift)
        alpha = jnp.exp(m_prev - m_new)
        p = jnp.exp(s - (m_new + shift))
        l_ref[...] = alpha * l_ref[...] + jnp.sum(p, axis=0, keepdims=True)
        acc_ref[...] = alpha * acc_ref[...] + _dot(vt_ref[:, pl.ds(off, tq)], p.astype(BF16))
        m_ref[...] = m_new

    def past(ki, carry):
        block(ki, bias_ref[...], slope * ((qi - ki) * tq).astype(F32))
        return carry

    lax.fori_loop(0, qi, past, 0)
    block(qi, jnp.where(d0 >= 0, bias_ref[...], -NEG_INF), 0.0)
    o = acc_ref[...] / l_ref[...]
    of = (o[:, :tq] - lam * o[:, tq:]).T
    o_ref[...] = _subln(of, g_ref[...], lam_init)


def _attn_prompt(scal, q, k, v, g, batch, seq, lam_init):
    tq = ATT_TILE
    nq = seq // tq
    grid_spec = pltpu.PrefetchScalarGridSpec(
        num_scalar_prefetch=1,
        grid=(batch, DA_HEADS, nq),
        in_specs=[pl.BlockSpec((tq, DA_VDIM), lambda b, h, i, s: (b * nq + i, h)),
                  pl.BlockSpec((seq, DA_VDIM), lambda b, h, i, s: (b, h)),
                  pl.BlockSpec((seq, DA_VDIM), lambda b, h, i, s: (b, h)),
                  pl.BlockSpec((1, DA_VDIM), lambda b, h, i, s: (0, 0))],
        out_specs=pl.BlockSpec((tq, DA_VDIM), lambda b, h, i, s: (b * nq + i, h)),
        scratch_shapes=[pltpu.VMEM((seq, DA_VDIM), BF16), pltpu.VMEM((DA_VDIM, seq), BF16),
                        pltpu.VMEM((tq, 2 * tq), F32),
                        pltpu.VMEM((1, 2 * tq), F32), pltpu.VMEM((1, 2 * tq), F32),
                        pltpu.VMEM((DA_VDIM, 2 * tq), F32)])
    return pl.pallas_call(
        functools.partial(_attn_prompt_kernel, lam_init=lam_init),
        grid_spec=grid_spec,
        out_shape=jax.ShapeDtypeStruct((batch * seq, D_MODEL), F32),
        compiler_params=_cparams(3),
    )(scal, q, k, v, g)


NEW_TOKENS_PAD = LANES // DA_HEADS


def _attn_sample_kernel(pt_ref, scal_ref, q_ref, *rest, dec_seq, page, lam_init):
    k_refs, v_refs = rest[:SAMPLE_PAGES], rest[SAMPLE_PAGES:2 * SAMPLE_PAGES]
    (kn_ref, vn_ref, bias_ref, biasn_ref, g_ref, o_ref, qs_ref, m_ref, l_ref, acc_ref
     ) = rest[2 * SAMPLE_PAGES:]
    step = pl.program_id(1)
    rows = 2 * dec_seq * DA_HEADS
    row_h = lax.broadcasted_iota(jnp.int32, (rows, 1), 0) % DA_HEADS
    slope = jnp.zeros((rows, 1), F32)
    for h in range(DA_HEADS):
        slope = jnp.where(row_h == h, scal_ref[h], slope)

    @pl.when(step == 0)
    def _():
        q = q_ref[0] * (DA_HEAD ** -0.5)
        lane = lax.broadcasted_iota(jnp.int32, q.shape, 1)
        qs_ref[...] = jnp.concatenate(
            [jnp.where(lane < DA_HEAD, q, 0.0), jnp.where(lane >= DA_HEAD, q, 0.0)],
            axis=0).astype(BF16)
        m_ref[...] = jnp.full_like(m_ref, -jnp.inf)
        l_ref[...] = jnp.zeros_like(l_ref)
        acc_ref[...] = jnp.zeros_like(acc_ref)

    def block(kb, vb, bias, shift):
        s = _dot_nt(qs_ref[...], kb) - bias
        m_prev = m_ref[...]
        m_new = jnp.maximum(m_prev, jnp.max(s, axis=-1, keepdims=True) + shift)
        alpha = jnp.exp(m_prev - m_new)
        pr = jnp.exp(s - (m_new - shift))
        l_ref[...] = alpha * l_ref[...] + jnp.sum(pr, axis=-1, keepdims=True)
        acc_ref[...] = alpha * acc_ref[...] + _dot(pr.astype(BF16), vb)
        m_ref[...] = m_new

    flat = lambda ref: ref[0].reshape(page * DA_HEADS, DA_VDIM).astype(BF16)
    for j in range(SAMPLE_PAGES):
        first_tok = ((step * SAMPLE_PAGES + j) * page).astype(F32)
        block(flat(k_refs[j]), flat(v_refs[j]), bias_ref[...], slope * first_tok)

    @pl.when(step == pl.num_programs(1) - 1)
    def _():
        block(kn_ref[0].astype(BF16), vn_ref[0].astype(BF16), biasn_ref[...], 0.0)
        o = acc_ref[...] / l_ref[...]
        of = o[:rows // 2] - scal_ref[DA_HEADS] * o[rows // 2:]
        o_ref[0] = _subln(of, g_ref[...], lam_init)


def _attn_sample(page_table, scal, q, cache_k, cache_v, k_new, v_new, g, lam_init):
    b = q.shape[0]
    dec_seq = q.shape[1] // DA_HEADS
    n_pages = page_table.shape[1]
    page = cache_k.shape[1]
    past_len = n_pages * page
    rows = 2 * dec_seq * DA_HEADS
    assert n_pages % SAMPLE_PAGES == 0 and dec_seq <= NEW_TOKENS_PAD

    slopes = scal[:DA_HEADS]
    r = jnp.arange(rows)
    r_q, r_h = (r // DA_HEADS) % dec_seq, r % DA_HEADS

    def bias_tile(n_tok, first_q_pos, causal):
        col = jnp.arange(n_tok * DA_HEADS)
        dist = first_q_pos + r_q[:, None] - (col // DA_HEADS)[None, :]
        ok = (col % DA_HEADS)[None, :] == r_h[:, None]
        if causal:
            ok = ok & (dist >= 0)
        return jnp.where(ok, slopes[r_h][:, None] * dist.astype(F32), -NEG_INF)

    bias_past = bias_tile(page, past_len, False)
    bias_new = bias_tile(NEW_TOKENS_PAD, 0, True)

    def page_spec(j):
        return pl.BlockSpec((1, page, DA_HEADS, DA_VDIM),
                            lambda i, p, pt, s: (pt[i * n_pages + p * SAMPLE_PAGES + j], 0, 0, 0))

    per_seq = lambda shape: pl.BlockSpec((1,) + shape, lambda i, p, pt, s: (i, 0, 0))
    const = lambda shape: pl.BlockSpec(shape, lambda i, p, pt, s: (0, 0))
    grid_spec = pltpu.PrefetchScalarGridSpec(
        num_scalar_prefetch=2,
        grid=(b, n_pages // SAMPLE_PAGES),
        in_specs=[per_seq((dec_seq * DA_HEADS, DA_VDIM))]
                 + [page_spec(j) for j in range(SAMPLE_PAGES)] * 2
                 + [per_seq((LANES, DA_VDIM)), per_seq((LANES, DA_VDIM)),
                    const(bias_past.shape), const(bias_new.shape), const((1, DA_VDIM))],
        out_specs=per_seq((dec_seq * DA_HEADS, DA_VDIM)),
        scratch_shapes=[pltpu.VMEM((rows, DA_VDIM), BF16),
                        pltpu.VMEM((rows, 1), F32), pltpu.VMEM((rows, 1), F32),
                        pltpu.VMEM((rows, DA_VDIM), F32)])
    return pl.pallas_call(
        functools.partial(_attn_sample_kernel, dec_seq=dec_seq, page=page, lam_init=lam_init),
        grid_spec=grid_spec,
        out_shape=jax.ShapeDtypeStruct(q.shape, F32),
        compiler_params=_cparams(2),
    )(page_table.reshape(-1), scal, q, *([cache_k] * SAMPLE_PAGES), *([cache_v] * SAMPLE_PAGES),
      k_new, v_new, bias_past, bias_new, g)


def _lambda_init(layer):
    return 0.8 - 0.6 * math.exp(-0.3 * layer)


def _peer_layer(x, layer, peer_w_q, peer_b_q, peer_subkeys, peer_u, peer_v, ln2_g, ln2_b):
    wq_t = peer_w_q[layer].T.astype(BF16)
    bq_b = jnp.broadcast_to(peer_b_q[layer][:, None], (PEER_HEADS * PEER_QDIM, ROUTE_TILE))
    xt, rk, e2, e1, nsel = _peer_route(x, wq_t, bq_b, peer_subkeys[layer].astype(BF16))
    vec = jnp.stack([ln2_g[layer], ln2_b[layer]])
    return _peer_sweep(x, xt, rk, e2, e1, nsel, peer_u[layer].astype(BF16),
                       peer_v[layer].T.astype(BF16), vec)


def kernel(x_prompt, x_sample, state_shift, state_wkv, cache_k, cache_v, page_table, p_prompt, p_sample, rw_mu, rw_w_r, rw_w_k, rw_w_v, rw_w0, rw_w1, rw_w2, rw_a0, rw_a1, rw_a2, rw_g1, rw_g2, rw_k_k, rw_k_a, rw_r_k, rw_gn_g, rw_gn_b, rw_w_o, da_w_k, da_w_v, da_w_q, da_lam, da_subln_g, da_w_o, ln1_g, ln1_b, ln2_g, ln2_b, peer_w_q, peer_b_q, peer_subkeys, peer_u, peer_v, ple_w_p, ple_w_g, ple_b_g):
    bp, tp, d = x_prompt.shape
    bs, ts, _ = x_sample.shape
    assert d == D_MODEL and bp * RW_HEADS == LANES and bs == LANES
    np_, ns = bp * tp, bs * ts
    hh, hd = RW_HEADS, RW_HEAD
    bf = lambda w: w.astype(BF16)

    x = jnp.concatenate([x_prompt.reshape(np_, d), x_sample.reshape(ns, d)], axis=0)
    n = np_ + ns
    assert n % PEER_TOK_TILE == 0

    xp_prev = jnp.concatenate([jnp.zeros((bp, 1, d), F32), x_prompt[:, :-1]], axis=1)
    xs_prev = jnp.concatenate([state_shift[0][:, None, :], x_sample[:, :-1]], axis=1)
    x_prev = jnp.concatenate([xp_prev.reshape(np_, d), xs_prev.reshape(ns, d)], axis=0)
    head_of_lane = jnp.arange(d) // hd
    ind = (head_of_lane[:, None] == jnp.arange(LANES)[None, :]).astype(BF16)
    ind_t = ind.T
    vec_pre = jnp.stack([rw_w0[0], rw_a0[0], rw_k_k[0], rw_k_a[0]])
    r, w, k, v, a, b, g = _rwkv_pre(
        x, x_prev, rw_mu[0], bf(rw_w_r[0]), bf(rw_w_k[0]), bf(rw_w_v[0]), bf(rw_w1[0]), bf(rw_w2[0]),
        bf(rw_a1[0]), bf(rw_a2[0]), bf(rw_g1[0]), bf(rw_g2[0]), vec_pre, ind, ind_t)

    to_p = lambda t: t[:np_].reshape(bp, tp, hh, hd).transpose(1, 3, 0, 2).reshape(1, tp, hd, LANES)
    to_s = lambda t: t[np_:].reshape(bs, ts, hh, hd).transpose(2, 1, 3, 0)
    y_p, st_p = _wkv_scan(*(to_p(t) for t in (r, w, k, v, a, b)),
                          jnp.zeros((1, hd, hd, LANES), F32))
    s0_s = state_wkv[0].astype(F32).transpose(1, 2, 3, 0)
    y_s, st_s = _wkv_scan(*(to_s(t) for t in (r, w, k, v, a, b)), s0_s)
    y = jnp.concatenate([
        y_p.reshape(tp, hd, bp, hh).transpose(2, 0, 3, 1).reshape(np_, d),
        y_s.transpose(3, 1, 0, 2).reshape(ns, d)], axis=0)
    wkv_p = st_p.reshape(hd, hd, bp, hh).transpose(2, 3, 0, 1)[None].astype(state_wkv.dtype)
    wkv_s = st_s.transpose(3, 0, 1, 2)[None].astype(state_wkv.dtype)
    shift_p = x_prompt[:, -1][None]
    shift_s = x_sample[:, -1][None]

    vec_post = jnp.stack([rw_gn_g[0], rw_gn_b[0], rw_r_k[0].reshape(d), ln1_g[0], ln1_b[0]])
    x = _rwkv_post(x, y, r, k, v, g, vec_post, ind, ind_t, bf(rw_w_o[0]))
    x = _peer_layer(x, 0, peer_w_q, peer_b_q, peer_subkeys, peer_u, peer_v, ln2_g, ln2_b)
    p_all = jnp.concatenate([p_prompt.reshape(DEPTH, np_, -1), p_sample.reshape(DEPTH, ns, -1)], axis=1)
    x, k_sh, v_sh, q = _ple(x, p_all[0], bf(ple_w_p[0]), bf(ple_w_g[0]), ple_b_g[0][None],
                            (bf(da_w_k), bf(da_w_v), bf(da_w_q[0])))

    lam_init = _lambda_init(1)
    lv = da_lam[0].astype(F32)
    lam = jnp.exp(jnp.sum(lv[0] * lv[1])) - jnp.exp(jnp.sum(lv[2] * lv[3])) + lam_init
    slopes = 2.0 ** (-8.0 * jnp.arange(1, DA_HEADS + 1, dtype=F32) / DA_HEADS)
    scal = jnp.concatenate([slopes, lam[None]]).astype(F32)
    o_p = _attn_prompt(scal, q[:np_], k_sh[:np_], v_sh[:np_], da_subln_g[0][None], bp, tp, lam_init)
    pad = lambda t: jnp.pad(t[np_:].reshape(bs, ts, d), ((0, 0), (0, NEW_TOKENS_PAD - ts), (0, 0))
                            ).reshape(bs, LANES, DA_VDIM)
    o_s = _attn_sample(page_table, scal, q[np_:].reshape(bs, ts * DA_HEADS, DA_VDIM),
                       cache_k, cache_v, pad(k_sh), pad(v_sh), da_subln_g[0][None], lam_init)
    o = jnp.concatenate([o_p, o_s.reshape(ns, d)], axis=0)
    x = _proj_ln(x, o, jnp.stack([ln1_g[1], ln1_b[1]]), bf(da_w_o[0]))
    x = _peer_layer(x, 1, peer_w_q, peer_b_q, peer_subkeys, peer_u, peer_v, ln2_g, ln2_b)
    (x,) = _ple(x, p_all[1], bf(ple_w_p[1]), bf(ple_w_g[1]), ple_b_g[1][None])

    kv_shape_p = (bp, tp, DA_HEADS, DA_VDIM)
    kv_shape_s = (bs, ts, DA_HEADS, DA_VDIM)
    return (x[:np_].reshape(bp, tp, d), x[np_:].reshape(bs, ts, d),
            shift_p, wkv_p,
            k_sh[:np_].reshape(kv_shape_p), v_sh[:np_].reshape(kv_shape_p),
            shift_s, wkv_s,
            k_sh[np_:].reshape(kv_shape_s), v_sh[np_:].reshape(kv_shape_s))
```

```python
import functools
import math

import jax
import jax.numpy as jnp
from jax import lax
from jax.experimental import pallas as pl
from jax.experimental.pallas import tpu as pltpu

F32 = jnp.float32
BF16 = jnp.bfloat16

D_MODEL = 1024
DEPTH = 2
RW_HEAD = 64
RW_HEADS = D_MODEL // RW_HEAD
RW_GN_EPS = 64e-5
DA_HEAD = 64
DA_HEADS = D_MODEL // (2 * DA_HEAD)
DA_VDIM = 2 * DA_HEAD
RMS_EPS = 1e-5
NEG_INF = -1e30
PEER_HEADS = 8
PEER_NKEYS = 128
PEER_NEXPERTS = PEER_NKEYS * PEER_NKEYS
PEER_QDIM = 256
PEER_HALF = PEER_QDIM // 2
PEER_TOPK = 16
DN_ALPHA = (2.0 * DEPTH) ** 0.25
LN_EPS = 1e-5

LANES = 128
BF16_ROWS = 16
VMEM_LIMIT = 56 * 1024 * 1024

ROW_TILE = 256
ROUTE_TILE = 256
PEER_TOK_TILE = 512
PEER_EXP_TILE = 1024
SCAN_TCHUNK = 32
SCAN_VROWS = 4
ATT_TILE = 512
SAMPLE_PAGES = 4


def _cparams(n_axes):
    return pltpu.CompilerParams(dimension_semantics=("arbitrary",) * n_axes,
                                vmem_limit_bytes=VMEM_LIMIT)


def _dot(a, b):
    return jnp.dot(a, b, preferred_element_type=F32)


def _dot_nt(a, b):
    return lax.dot_general(a, b, (((1,), (1,)), ((), ())), preferred_element_type=F32)


def _layer_norm(x, g, b):
    mu = jnp.mean(x, axis=-1, keepdims=True)
    xc = x - mu
    var = jnp.mean(xc * xc, axis=-1, keepdims=True)
    return xc * lax.rsqrt(var + LN_EPS) * g + b


def _split2(x):
    hi = x.astype(BF16)
    lo = (x - hi.astype(F32)).astype(BF16)
    return hi, lo


def _head_sum(x, ind, ind_t):
    hi, lo = _split2(x)
    s = _dot(hi, ind) + _dot(lo, ind)
    shi, slo = _split2(s)
    return _dot(shi, ind_t) + _dot(slo, ind_t)


def _full(shape):
    n = len(shape)
    return pl.BlockSpec(shape, lambda *_: (0,) * n)


def _rwkv_pre_kernel(x_ref, xp_ref, mu_ref, wr_ref, wk_ref, wv_ref, w1_ref, w2_ref, a1_ref, a2_ref,
                     g1_ref, g2_ref, vec_ref, ind_ref, indt_ref,
                     r_ref, w_ref, k_ref, v_ref, a_ref, b_ref, g_ref):
    x = x_ref[...]
    xx = xp_ref[...] - x
    mix = lambda i: (x + xx * mu_ref[i:i + 1, :]).astype(BF16)
    xr, xw, xk, xv, xa, xg = (mix(i) for i in range(6))
    w0, a0, k_k, k_a = (vec_ref[i:i + 1, :] for i in range(4))
    r = _dot(xr, wr_ref[...])
    k = _dot(xk, wk_ref[...])
    v = _dot(xv, wv_ref[...])
    wl = w0 + _dot(jnp.tanh(_dot(xw, w1_ref[...])).astype(BF16), w2_ref[...])
    z = -wl
    softplus = jnp.maximum(z, 0.0) + jnp.log(1.0 + jnp.exp(-jnp.abs(z)))
    decay = jnp.exp(-jnp.exp(-softplus - 0.5))
    a = jax.nn.sigmoid(a0 + _dot(_dot(xa, a1_ref[...]).astype(BF16), a2_ref[...]))
    g = _dot(jax.nn.sigmoid(_dot(xg, g1_ref[...])).astype(BF16), g2_ref[...])
    kk = k * k_k
    n2 = _head_sum(kk * kk, ind_ref[...], indt_ref[...])
    kk = kk / jnp.maximum(jnp.sqrt(n2), 1e-12)
    r_ref[...] = r
    w_ref[...] = decay
    k_ref[...] = k * (1.0 + (a - 1.0) * k_a)
    v_ref[...] = v
    a_ref[...] = -kk
    b_ref[...] = kk * a
    g_ref[...] = g


def _rwkv_pre(x, xp, mu, w_r, w_k, w_v, w1, w2, a1, a2, g1, g2, vec, ind, ind_t):
    n = x.shape[0]
    tm = ROW_TILE
    row = pl.BlockSpec((tm, D_MODEL), lambda i: (i, 0))
    consts = (mu, w_r, w_k, w_v, w1, w2, a1, a2, g1, g2, vec, ind, ind_t)
    return pl.pallas_call(
        _rwkv_pre_kernel,
        grid=(n // tm,),
        in_specs=[row, row] + [_full(c.shape) for c in consts],
        out_specs=[row] * 7,
        out_shape=[jax.ShapeDtypeStruct((n, D_MODEL), F32)] * 7,
        compiler_params=_cparams(1),
    )(x, xp, *consts)


def _scan_kernel(r_ref, w_ref, k_ref, v_ref, a_ref, b_ref, s0_ref, y_ref, st_ref, wr_ref, dot_ref,
                 *, tchunk):
    @pl.when(pl.program_id(1) == 0)
    def _():
        st_ref[...] = s0_ref[...]

    def prep(t, carry):
        r = r_ref[0, t]
        wr_ref[t] = w_ref[0, t] * r
        dot_ref[t, 0:1, :] = jnp.sum(b_ref[0, t] * r, axis=0, keepdims=True)
        dot_ref[t, 1:2, :] = jnp.sum(k_ref[0, t] * r, axis=0, keepdims=True)
        return carry

    lax.fori_loop(0, tchunk, prep, 0)

    def vloop(vc, carry):
        v0 = vc * SCAN_VROWS
        state = tuple(st_ref[0, v0 + j] for j in range(SCAN_VROWS))

        def step(t, st):
            a = a_ref[0, t]
            wr = wr_ref[t]
            w = w_ref[0, t]
            b = b_ref[0, t]
            k = k_ref[0, t]
            br = dot_ref[t, 0:1, :]
            kr = dot_ref[t, 1:2, :]
            out = []
            for j in range(SCAN_VROWS):
                s = st[j]
                val = v_ref[0, t, pl.ds(v0 + j, 1), :]
                sa = jnp.sum(s * a, axis=0, keepdims=True)
                y0 = jnp.sum(s * wr, axis=0, keepdims=True)
                y_ref[0, t, pl.ds(v0 + j, 1), :] = y0 + sa * br + val * kr
                out.append(s * w + sa * b + val * k)
            return tuple(out)

        state = lax.fori_loop(0, tchunk, step, state)
        for j in range(SCAN_VROWS):
            st_ref[0, v0 + j] = state[j]
        return carry

    lax.fori_loop(0, RW_HEAD // SCAN_VROWS, vloop, 0)


def _wkv_scan(r, w, k, v, a, b, s0):
    g, t = r.shape[0], r.shape[1]
    tchunk = min(SCAN_TCHUNK, t)
    seq = pl.BlockSpec((1, tchunk, RW_HEAD, LANES), lambda gi, ti: (gi, ti, 0, 0))
    st = pl.BlockSpec((1, RW_HEAD, RW_HEAD, LANES), lambda gi, ti: (gi, 0, 0, 0))
    return pl.pallas_call(
        functools.partial(_scan_kernel, tchunk=tchunk),
        grid=(g, t // tchunk),
        in_specs=[seq] * 6 + [st],
        out_specs=[seq, st],
        out_shape=[jax.ShapeDtypeStruct(r.shape, F32), jax.ShapeDtypeStruct(s0.shape, F32)],
        scratch_shapes=[pltpu.VMEM((tchunk, RW_HEAD, LANES), F32), pltpu.VMEM((tchunk, 8, LANES), F32)],
        compiler_params=_cparams(2),
    )(r, w, k, v, a, b, s0)


def _rwkv_post_kernel(x_ref, y_ref, r_ref, k_ref, v_ref, g_ref, vec_ref, ind_ref, indt_ref, wo_ref,
                      o_ref):
    ind, ind_t = ind_ref[...], indt_ref[...]
    gn_g, gn_b, r_k, ln_g, ln_b = (vec_ref[i:i + 1, :] for i in range(5))
    y = y_ref[...]
    inv = 1.0 / RW_HEAD
    yc = y - _head_sum(y, ind, ind_t) * inv
    var = _head_sum(yc * yc, ind, ind_t) * inv
    yn = yc * lax.rsqrt(var + RW_GN_EPS) * gn_g + gn_b
    v = v_ref[...]
    bonus = _head_sum(r_ref[...] * k_ref[...] * r_k, ind, ind_t) * v
    h = _dot(((yn + bonus) * g_ref[...]).astype(BF16), wo_ref[...])
    o_ref[...] = _layer_norm(DN_ALPHA * x_ref[...] + h, ln_g, ln_b)


def _rwkv_post(x, y, r, k, v, g, vec, ind, ind_t, w_o):
    n = x.shape[0]
    tm = ROW_TILE
    row = pl.BlockSpec((tm, D_MODEL), lambda i: (i, 0))
    consts = (vec, ind, ind_t, w_o)
    return pl.pallas_call(
        _rwkv_post_kernel,
        grid=(n // tm,),
        in_specs=[row] * 6 + [_full(c.shape) for c in consts],
        out_specs=row,
        out_shape=jax.ShapeDtypeStruct((n, D_MODEL), F32),
        compiler_params=_cparams(1),
    )(x, y, r, k, v, g, *consts)


def _proj_ln_kernel(x_ref, h_ref, vec_ref, wo_ref, o_ref):
    h = _dot(h_ref[...].astype(BF16), wo_ref[...])
    o_ref[...] = _layer_norm(DN_ALPHA * x_ref[...] + h, vec_ref[0:1, :], vec_ref[1:2, :])


def _proj_ln(x, h, vec, w_o):
    n = x.shape[0]
    tm = ROW_TILE
    row = pl.BlockSpec((tm, D_MODEL), lambda i: (i, 0))
    return pl.pallas_call(
        _proj_ln_kernel,
        grid=(n // tm,),
        in_specs=[row, row, _full(vec.shape), _full(w_o.shape)],
        out_specs=row,
        out_shape=jax.ShapeDtypeStruct((n, D_MODEL), F32),
        compiler_params=_cparams(1),
    )(x, h, vec, w_o)


NO_RANK = 127.0


def _bf16_pair_words(x):
    bits = lax.bitcast_convert_type(x.astype(BF16).astype(F32), jnp.int32)
    return bits | lax.shift_right_logical(bits, jnp.full_like(bits, 16))


def _peer_route_kernel(x_ref, wq_ref, bq_ref, sk_ref, xt_ref, rk_ref, e2_ref, e1_ref, ns_ref,
                       top_ref, cnt_ref):
    xt = x_ref[...].T.astype(BF16)
    xt_ref[...] = xt
    q = _dot(wq_ref[...], xt) + bq_ref[...]
    for h in range(PEER_HEADS):
        sc = []
        rank = None
        for c in range(2):
            lo = (2 * h + c) * PEER_HALF
            s = _dot(sk_ref[c], q[lo:lo + PEER_HALF, :].astype(BF16))
            sc.append(s)
            vals = s
            rank = jnp.full_like(s, NO_RANK)
            for j in range(PEER_TOPK):
                m = jnp.max(vals, axis=0, keepdims=True)
                hit = vals >= m
                top_ref[c, j:j + 1, :] = m
                cnt_ref[c, j:j + 1, :] = jnp.sum(jnp.where(hit, 1.0, 0.0), axis=0, keepdims=True)
                vals = jnp.where(hit, -jnp.inf, vals)
                if c == 1:
                    rank = jnp.where(hit, float(j), rank)
        ta, tb = top_ref[0], top_ref[1]
        ca, cb = cnt_ref[0], cnt_ref[1]
        half = PEER_TOPK // 2
        cand = [ta[0:1, :] + tb[:half], ta[0:1, :] + tb[half:]]
        mult = [ca[0:1, :] * cb[:half], ca[0:1, :] * cb[half:]]
        for i in range(1, PEER_TOPK):
            cand.append(ta[i:i + 1, :] + tb[:half])
            mult.append(ca[i:i + 1, :] * cb[:half])
        best = thr = zsum = None
        remaining = jnp.full((1, xt.shape[1]), float(PEER_TOPK), F32)
        for j in range(PEER_TOPK):
            m = jnp.max(functools.reduce(jnp.maximum, cand), axis=0, keepdims=True)
            hits = [cj >= m for cj in cand]
            n_hit = functools.reduce(jnp.add, [jnp.where(hj, mj, 0.0) for hj, mj in zip(hits, mult)])
            take = jnp.minimum(jnp.sum(n_hit, axis=0, keepdims=True), remaining)
            if j == 0:
                best, thr, zsum = m, m, take
            else:
                zsum = zsum + take * jnp.exp(m - best)
                thr = jnp.where(take > 0.0, m, thr)
            remaining = remaining - take
            cand = [jnp.where(hj, -jnp.inf, cj) for hj, cj in zip(hits, cand)]
        s1, s2 = sc
        nsel = jnp.zeros_like(s1)
        for j in range(PEER_TOPK):
            nsel = nsel + jnp.where(s1 + tb[j:j + 1, :] >= thr, 1.0, 0.0)
        rk_ref[h] = rank.astype(BF16)
        ns_ref[h] = _bf16_pair_words(nsel)
        e1_ref[h] = _bf16_pair_words(jnp.exp(s1 - ta[0:1, :]))
        e2_ref[h] = (jnp.exp(s2 - tb[0:1, :]) / zsum).astype(BF16)


def _peer_route(x, wq_t, bq_b, subkeys):
    n = x.shape[0]
    tb = ROUTE_TILE
    tab = pl.BlockSpec((PEER_HEADS, PEER_NKEYS, tb), lambda i: (0, 0, i))
    tab_words = jax.ShapeDtypeStruct((PEER_HEADS, PEER_NKEYS, n), jnp.int32)
    tab_bf16 = jax.ShapeDtypeStruct((PEER_HEADS, PEER_NKEYS, n), BF16)
    return pl.pallas_call(
        _peer_route_kernel,
        grid=(n // tb,),
        in_specs=[pl.BlockSpec((tb, D_MODEL), lambda i: (i, 0)),
                  _full(wq_t.shape), _full(bq_b.shape), _full(subkeys.shape)],
        out_specs=[pl.BlockSpec((D_MODEL, tb), lambda i: (0, i)), tab, tab, tab, tab],
        out_shape=[jax.ShapeDtypeStruct((D_MODEL, n), BF16), tab_bf16, tab_bf16, tab_words, tab_words],
        scratch_shapes=[pltpu.VMEM((2, PEER_TOPK, tb), F32), pltpu.VMEM((2, PEER_TOPK, tb), F32)],
        compiler_params=_cparams(1),
    )(x, wq_t, bq_b, subkeys)


def _gelu_tanh(x):
    c0 = math.sqrt(2.0 / math.pi)
    inner = x * (jnp.asarray(c0, x.dtype) + jnp.asarray(c0 * 0.044715, x.dtype) * (x * x))
    hx = jnp.asarray(0.5, x.dtype) * x
    return hx + hx * jnp.tanh(inner)


def _peer_sweep_kernel(x_ref, xt_ref, rk_ref, e2_ref, e1_ref, ns_ref, u_ref, vt_ref, vec_ref,
                       o_ref, acc_ref):
    e = pl.program_id(1)

    @pl.when(e == 0)
    def _():
        acc_ref[...] = jnp.zeros_like(acc_ref)

    n_sub = PEER_EXP_TILE // PEER_NKEYS
    tb = xt_ref.shape[1]
    zero = jnp.zeros((), BF16)
    first = pl.multiple_of(e * n_sub, n_sub)

    def row_tile(ref, h, j):
        row = ref[h, pl.ds(first, n_sub), :][j:j + 1, :]
        tile = pltpu.bitcast(jnp.broadcast_to(row, (BF16_ROWS // 2, tb)), BF16)
        return jnp.concatenate([tile] * (PEER_NKEYS // BF16_ROWS), axis=0)

    acts = []
    for j in range(n_sub):
        rows = slice(j * PEER_NKEYS, (j + 1) * PEER_NKEYS)
        gate = None
        for h in range(PEER_HEADS):
            keep = rk_ref[h] < row_tile(ns_ref, h, j)
            term = jnp.where(keep, e2_ref[h], zero) * row_tile(e1_ref, h, j)
            gate = term if gate is None else gate + term
        ht = _dot(u_ref[rows, :], xt_ref[...])
        acts.append(gate * _gelu_tanh(ht.astype(BF16)))
    acc_ref[...] += _dot(vt_ref[...], jnp.concatenate(acts, axis=0))

    @pl.when(e == pl.num_programs(1) - 1)
    def _():
        c = acc_ref[...].T
        o_ref[...] = _layer_norm(DN_ALPHA * x_ref[...] + c, vec_ref[0:1, :], vec_ref[1:2, :])


def _peer_sweep(x, xt, rk, e2, e1, nsel, u_bf, vt_bf, vec):
    n = x.shape[0]
    tb, eb = PEER_TOK_TILE, PEER_EXP_TILE
    tab = pl.BlockSpec((PEER_HEADS, PEER_NKEYS, tb), lambda i, e: (0, 0, i))
    return pl.pallas_call(
        _peer_sweep_kernel,
        grid=(n // tb, PEER_NEXPERTS // eb),
        in_specs=[pl.BlockSpec((tb, D_MODEL), lambda i, e: (i, 0)),
                  pl.BlockSpec((D_MODEL, tb), lambda i, e: (0, i)),
                  tab, tab, tab, tab,
                  pl.BlockSpec((eb, D_MODEL), lambda i, e: (e, 0)),
                  pl.BlockSpec((D_MODEL, eb), lambda i, e: (0, e)),
                  pl.BlockSpec(vec.shape, lambda i, e: (0, 0))],
        out_specs=pl.BlockSpec((tb, D_MODEL), lambda i, e: (i, 0)),
        out_shape=jax.ShapeDtypeStruct((n, D_MODEL), F32),
        scratch_shapes=[pltpu.VMEM((D_MODEL, tb), F32)],
        compiler_params=_cparams(2),
    )(x, xt, rk, e2, e1, nsel, u_bf, vt_bf, vec)


def _ple_kernel(x_ref, p_ref, wp_ref, wg_ref, bg_ref, *rest, n_extra):
    we_refs, o_ref, e_refs = rest[:n_extra], rest[n_extra], rest[n_extra + 1:]
    x = x_ref[...]
    gate = jax.nn.sigmoid(_dot(x.astype(BF16), wg_ref[...]) + bg_ref[...])
    x3 = x + _dot(p_ref[...].astype(BF16), wp_ref[...]) * gate
    o_ref[...] = x3
    x3b = x3.astype(BF16)
    for we_ref, e_ref in zip(we_refs, e_refs):
        e_ref[...] = _dot(x3b, we_ref[...])


def _ple(x, p, w_p, w_g, b_g, w_extra=()):
    n = x.shape[0]
    tm = ROW_TILE
    row = pl.BlockSpec((tm, D_MODEL), lambda i: (i, 0))
    n_extra = len(w_extra)
    return pl.pallas_call(
        functools.partial(_ple_kernel, n_extra=n_extra),
        grid=(n // tm,),
        in_specs=[row, pl.BlockSpec((tm, p.shape[1]), lambda i: (i, 0)),
                  _full(w_p.shape), _full(w_g.shape), _full(b_g.shape)]
                 + [_full(w.shape) for w in w_extra],
        out_specs=[row] * (1 + n_extra),
        out_shape=[jax.ShapeDtypeStruct((n, D_MODEL), F32)] * (1 + n_extra),
        compiler_params=_cparams(1),
    )(x, p, w_p, w_g, b_g, *w_extra)


def _subln(of, g, lam_init):
    of = of * lax.rsqrt(jnp.mean(of * of, axis=-1, keepdims=True) + RMS_EPS) * g
    return of * (1.0 - lam_init)


def _attn_prompt_kernel(scal_ref, q_ref, k_ref, v_ref, g_ref, o_ref, kb_ref, vt_ref, bias_ref,
                        m_ref, l_ref, acc_ref, *, lam_init):
    h = pl.program_id(1)
    qi = pl.program_id(2)
    tq = ATT_TILE
    slope = scal_ref[h]
    lam = scal_ref[DA_HEADS]
    d0 = (lax.broadcasted_iota(jnp.int32, (tq, 2 * tq), 1) % tq
          - lax.broadcasted_iota(jnp.int32, (tq, 2 * tq), 0))

    @pl.when(qi == 0)
    def _():
        kb_ref[...] = k_ref[...].astype(BF16)
        vt_ref[...] = v_ref[...].T.astype(BF16)
        bias_ref[...] = slope * d0.astype(F32)

    qt = (q_ref[...] * (DA_HEAD ** -0.5)).T
    sub = lax.broadcasted_iota(jnp.int32, qt.shape, 0)
    qs = jnp.concatenate([jnp.where(sub < DA_HEAD, qt, 0.0), jnp.where(sub >= DA_HEAD, qt, 0.0)],
                         axis=1).astype(BF16)
    m_ref[...] = jnp.full_like(m_ref, -jnp.inf)
    l_ref[...] = jnp.zeros_like(l_ref)
    acc_ref[...] = jnp.zeros_like(acc_ref)

    def block(ki, bias, shift):
        off = pl.multiple_of(ki * tq, tq)
        s = _dot(kb_ref[pl.ds(off, tq), :], qs) - bias
        m_prev = m_ref[...]
        m_new = jnp.maximum(m_prev, jnp.max(s, axis=0, keepdims=True) - shift)
        alpha = jnp.exp(m_prev - m_new)
        p = jnp.exp(s - (m_new + shift))
        l_ref[...] = alpha * l_ref[...] + jnp.sum(p, axis=0, keepdims=True)
        acc_ref[...] = alpha * acc_ref[...] + _dot(vt_ref[:, pl.ds(off, tq)], p.astype(BF16))
        m_ref[...] = m_new

    def past(ki, carry):
        block(ki, bias_ref[...], slope * ((qi - ki) * tq).astype(F32))
        return carry

    lax.fori_loop(0, qi, past, 0)
    block(qi, jnp.where(d0 >= 0, bias_ref[...], -NEG_INF), 0.0)
    o = acc_ref[...] / l_ref[...]
    of = (o[:, :tq] - lam * o[:, tq:]).T
    o_ref[...] = _subln(of, g_ref[...], lam_init)


def _attn_prompt(scal, q, k, v, g, batch, seq, lam_init):
    tq = ATT_TILE
    nq = seq // tq
    grid_spec = pltpu.PrefetchScalarGridSpec(
        num_scalar_prefetch=1,
        grid=(batch, DA_HEADS, nq),
        in_specs=[pl.BlockSpec((tq, DA_VDIM), lambda b, h, i, s: (b * nq + i, h)),
                  pl.BlockSpec((seq, DA_VDIM), lambda b, h, i, s: (b, h)),
                  pl.BlockSpec((seq, DA_VDIM), lambda b, h, i, s: (b, h)),
                  pl.BlockSpec((1, DA_VDIM), lambda b, h, i, s: (0, 0))],
        out_specs=pl.BlockSpec((tq, DA_VDIM), lambda b, h, i, s: (b * nq + i, h)),
        scratch_shapes=[pltpu.VMEM((seq, DA_VDIM), BF16), pltpu.VMEM((DA_VDIM, seq), BF16),
                        pltpu.VMEM((tq, 2 * tq), F32),
                        pltpu.VMEM((1, 2 * tq), F32), pltpu.VMEM((1, 2 * tq), F32),
                        pltpu.VMEM((DA_VDIM, 2 * tq), F32)])
    return pl.pallas_call(
        functools.partial(_attn_prompt_kernel, lam_init=lam_init),
        grid_spec=grid_spec,
        out_shape=jax.ShapeDtypeStruct((batch * seq, D_MODEL), F32),
        compiler_params=_cparams(3),
    )(scal, q, k, v, g)


NEW_TOKENS_PAD = LANES // DA_HEADS


def _attn_sample_kernel(pt_ref, scal_ref, q_ref, *rest, dec_seq, page, lam_init):
    k_refs, v_refs = rest[:SAMPLE_PAGES], rest[SAMPLE_PAGES:2 * SAMPLE_PAGES]
    (kn_ref, vn_ref, bias_ref, biasn_ref, g_ref, o_ref, qs_ref, m_ref, l_ref, acc_ref
     ) = rest[2 * SAMPLE_PAGES:]
    step = pl.program_id(1)
    rows = 2 * dec_seq * DA_HEADS
    row_h = lax.broadcasted_iota(jnp.int32, (rows, 1), 0) % DA_HEADS
    slope = jnp.zeros((rows, 1), F32)
    for h in range(DA_HEADS):
        slope = jnp.where(row_h == h, scal_ref[h], slope)

    @pl.when(step == 0)
    def _():
        q = q_ref[0] * (DA_HEAD ** -0.5)
        lane = lax.broadcasted_iota(jnp.int32, q.shape, 1)
        qs_ref[...] = jnp.concatenate(
            [jnp.where(lane < DA_HEAD, q, 0.0), jnp.where(lane >= DA_HEAD, q, 0.0)],
            axis=0).astype(BF16)
        m_ref[...] = jnp.full_like(m_ref, -jnp.inf)
        l_ref[...] = jnp.zeros_like(l_ref)
        acc_ref[...] = jnp.zeros_like(acc_ref)

    def update(scores, values):
        m_prev = m_ref[...]
        m_new = functools.reduce(jnp.maximum,
                                 [jnp.max(s, axis=-1, keepdims=True) for s in scores], m_prev)
        alpha = jnp.exp(m_prev - m_new)
        l_new = alpha * l_ref[...]
        acc = alpha * acc_ref[...]
        for s, vb in zip(scores, values):
            pr = jnp.exp(s - m_new)
            l_new = l_new + jnp.sum(pr, axis=-1, keepdims=True)
            acc = acc + _dot(pr.astype(BF16), vb)
        m_ref[...] = m_new
        l_ref[...] = l_new
        acc_ref[...] = acc

    flat = lambda ref: ref[0].reshape(page * DA_HEADS, DA_VDIM).astype(BF16)
    scores = []
    for j in range(SAMPLE_PAGES):
        first_tok = ((step * SAMPLE_PAGES + j) * page).astype(F32)
        scores.append(_dot_nt(qs_ref[...], flat(k_refs[j])) - bias_ref[...] + slope * first_tok)
    update(scores, [flat(v_ref) for v_ref in v_refs])

    @pl.when(step == pl.num_programs(1) - 1)
    def _():
        update([_dot_nt(qs_ref[...], kn_ref[0].astype(BF16)) - biasn_ref[...]],
               [vn_ref[0].astype(BF16)])
        o = acc_ref[...] / l_ref[...]
        of = o[:rows // 2] - scal_ref[DA_HEADS] * o[rows // 2:]
        o_ref[0] = _subln(of, g_ref[...], lam_init)


def _attn_sample(page_table, scal, q, cache_k, cache_v, k_new, v_new, g, lam_init):
    b = q.shape[0]
    dec_seq = q.shape[1] // DA_HEADS
    n_pages = page_table.shape[1]
    page = cache_k.shape[1]
    past_len = n_pages * page
    rows = 2 * dec_seq * DA_HEADS
    assert n_pages % SAMPLE_PAGES == 0 and dec_seq <= NEW_TOKENS_PAD

    slopes = scal[:DA_HEADS]
    r = jnp.arange(rows)
    r_q, r_h = (r // DA_HEADS) % dec_seq, r % DA_HEADS

    def bias_tile(n_tok, first_q_pos, causal):
        col = jnp.arange(n_tok * DA_HEADS)
        dist = first_q_pos + r_q[:, None] - (col // DA_HEADS)[None, :]
        ok = (col % DA_HEADS)[None, :] == r_h[:, None]
        if causal:
            ok = ok & (dist >= 0)
        return jnp.where(ok, slopes[r_h][:, None] * dist.astype(F32), -NEG_INF)

    bias_past = bias_tile(page, past_len, False)
    bias_new = bias_tile(NEW_TOKENS_PAD, 0, True)

    def page_spec(j):
        return pl.BlockSpec((1, page, DA_HEADS, DA_VDIM),
                            lambda i, p, pt, s: (pt[i * n_pages + p * SAMPLE_PAGES + j], 0, 0, 0))

    per_seq = lambda shape: pl.BlockSpec((1,) + shape, lambda i, p, pt, s: (i, 0, 0))
    const = lambda shape: pl.BlockSpec(shape, lambda i, p, pt, s: (0, 0))
    grid_spec = pltpu.PrefetchScalarGridSpec(
        num_scalar_prefetch=2,
        grid=(b, n_pages // SAMPLE_PAGES),
        in_specs=[per_seq((dec_seq * DA_HEADS, DA_VDIM))]
                 + [page_spec(j) for j in range(SAMPLE_PAGES)] * 2
                 + [per_seq((LANES, DA_VDIM)), per_seq((LANES, DA_VDIM)),
                    const(bias_past.shape), const(bias_new.shape), const((1, DA_VDIM))],
        out_specs=per_seq((dec_seq * DA_HEADS, DA_VDIM)),
        scratch_shapes=[pltpu.VMEM((rows, DA_VDIM), BF16),
                        pltpu.VMEM((rows, 1), F32), pltpu.VMEM((rows, 1), F32),
                        pltpu.VMEM((rows, DA_VDIM), F32)])
    return pl.pallas_call(
        functools.partial(_attn_sample_kernel, dec_seq=dec_seq, page=page, lam_init=lam_init),
        grid_spec=grid_spec,
        out_shape=jax.ShapeDtypeStruct(q.shape, F32),
        compiler_params=_cparams(2),
    )(page_table.reshape(-1), scal, q, *([cache_k] * SAMPLE_PAGES), *([cache_v] * SAMPLE_PAGES),
      k_new, v_new, bias_past, bias_new, g)


def _lambda_init(layer):
    return 0.8 - 0.6 * math.exp(-0.3 * layer)


def _peer_weights(layer, peer_w_q, peer_b_q, peer_subkeys, peer_u, peer_v, ln2_g, ln2_b):
    wq_t = peer_w_q[layer].T.astype(BF16)
    bq_b = jnp.broadcast_to(peer_b_q[layer][:, None], (PEER_HEADS * PEER_QDIM, ROUTE_TILE))
    return (wq_t, bq_b, peer_subkeys[layer].astype(BF16), peer_u[layer].astype(BF16),
            peer_v[layer].T.astype(BF16), jnp.stack([ln2_g[layer], ln2_b[layer]]))


def _peer_layer(x, wq_t, bq_b, subkeys, u_bf, vt_bf, ln_vec):
    xt, rk, e2, e1, nsel = _peer_route(x, wq_t, bq_b, subkeys)
    return _peer_sweep(x, xt, rk, e2, e1, nsel, u_bf, vt_bf, ln_vec)


def kernel(x_prompt, x_sample, state_shift, state_wkv, cache_k, cache_v, page_table, p_prompt, p_sample, rw_mu, rw_w_r, rw_w_k, rw_w_v, rw_w0, rw_w1, rw_w2, rw_a0, rw_a1, rw_a2, rw_g1, rw_g2, rw_k_k, rw_k_a, rw_r_k, rw_gn_g, rw_gn_b, rw_w_o, da_w_k, da_w_v, da_w_q, da_lam, da_subln_g, da_w_o, ln1_g, ln1_b, ln2_g, ln2_b, peer_w_q, peer_b_q, peer_subkeys, peer_u, peer_v, ple_w_p, ple_w_g, ple_b_g):
    bp, tp, d = x_prompt.shape
    bs, ts, _ = x_sample.shape
    assert d == D_MODEL and bp * RW_HEADS == LANES and bs == LANES
    np_, ns = bp * tp, bs * ts
    hh, hd = RW_HEADS, RW_HEAD
    bf = lambda w: w.astype(BF16)

    head_of_lane = jnp.arange(d) // hd
    ind = (head_of_lane[:, None] == jnp.arange(LANES)[None, :]).astype(BF16)
    ind_t = ind.T
    pre_consts = (rw_mu[0], bf(rw_w_r[0]), bf(rw_w_k[0]), bf(rw_w_v[0]), bf(rw_w1[0]), bf(rw_w2[0]),
                  bf(rw_a1[0]), bf(rw_a2[0]), bf(rw_g1[0]), bf(rw_g2[0]),
                  jnp.stack([rw_w0[0], rw_a0[0], rw_k_k[0], rw_k_a[0]]), ind, ind_t)
    post_consts = (jnp.stack([rw_gn_g[0], rw_gn_b[0], rw_r_k[0].reshape(d), ln1_g[0], ln1_b[0]]),
                   ind, ind_t, bf(rw_w_o[0]))
    peer_consts = [_peer_weights(layer, peer_w_q, peer_b_q, peer_subkeys, peer_u, peer_v, ln2_g, ln2_b)
                   for layer in range(DEPTH)]
    ple_consts = [(bf(ple_w_p[layer]), bf(ple_w_g[layer]), ple_b_g[layer][None])
                  for layer in range(DEPTH)]
    kvq_w = (bf(da_w_k), bf(da_w_v), bf(da_w_q[0]))
    attn_out = (jnp.stack([ln1_g[1], ln1_b[1]]), bf(da_w_o[0]))
    lam_init = _lambda_init(1)
    lv = da_lam[0].astype(F32)
    lam = jnp.exp(jnp.sum(lv[0] * lv[1])) - jnp.exp(jnp.sum(lv[2] * lv[3])) + lam_init
    slopes = 2.0 ** (-8.0 * jnp.arange(1, DA_HEADS + 1, dtype=F32) / DA_HEADS)
    scal = jnp.concatenate([slopes, lam[None]]).astype(F32)
    subln_g = da_subln_g[0][None]

    def trunk(x, x_prev, p, to_chains, from_chains, s0, attend):
        r, w, k, v, a, b, g = _rwkv_pre(x, x_prev, *pre_consts)
        y, state = _wkv_scan(*(to_chains(t) for t in (r, w, k, v, a, b)), s0)
        x = _rwkv_post(x, from_chains(y), r, k, v, g, *post_consts)
        x = _peer_layer(x, *peer_consts[0])
        x, k_sh, v_sh, q = _ple(x, p[0], *ple_consts[0], kvq_w)
        x = _proj_ln(x, attend(q, k_sh, v_sh), *attn_out)
        x = _peer_layer(x, *peer_consts[1])
        (x,) = _ple(x, p[1], *ple_consts[1])
        return x, state, k_sh, v_sh

    xp_prev = jnp.concatenate([jnp.zeros((bp, 1, d), F32), x_prompt[:, :-1]], axis=1)
    y_prompt, st_p, k_p, v_p = trunk(
        x_prompt.reshape(np_, d), xp_prev.reshape(np_, d), p_prompt.reshape(DEPTH, np_, -1),
        lambda t: t.reshape(bp, tp, hh, hd).transpose(1, 3, 0, 2).reshape(1, tp, hd, LANES),
        lambda y: y.reshape(tp, hd, bp, hh).transpose(2, 0, 3, 1).reshape(np_, d),
        jnp.zeros((1, hd, hd, LANES), F32),
        lambda q, k_sh, v_sh: _attn_prompt(scal, q, k_sh, v_sh, subln_g, bp, tp, lam_init))

    xs_prev = jnp.concatenate([state_shift[0][:, None, :], x_sample[:, :-1]], axis=1)
    new_rows = lambda t: jnp.pad(t.reshape(bs, ts, d), ((0, 0), (0, NEW_TOKENS_PAD - ts), (0, 0))
                                 ).reshape(bs, LANES, DA_VDIM)
    y_sample, st_s, k_s, v_s = trunk(
        x_sample.reshape(ns, d), xs_prev.reshape(ns, d), p_sample.reshape(DEPTH, ns, -1),
        lambda t: t.reshape(bs, ts, hh, hd).transpose(2, 1, 3, 0),
        lambda y: y.transpose(3, 1, 0, 2).reshape(ns, d),
        state_wkv[0].astype(F32).transpose(1, 2, 3, 0),
        lambda q, k_sh, v_sh: _attn_sample(
            page_table, scal, q.reshape(bs, ts * DA_HEADS, DA_VDIM), cache_k, cache_v,
            new_rows(k_sh), new_rows(v_sh), subln_g, lam_init).reshape(ns, d))

    wkv_p = st_p.reshape(hd, hd, bp, hh).transpose(2, 3, 0, 1)[None].astype(state_wkv.dtype)
    wkv_s = st_s.transpose(3, 0, 1, 2)[None].astype(state_wkv.dtype)
    kv_shape_p = (bp, tp, DA_HEADS, DA_VDIM)
    kv_shape_s = (bs, ts, DA_HEADS, DA_VDIM)
    return (y_prompt.reshape(bp, tp, d), y_sample.reshape(bs, ts, d),
            x_prompt[:, -1][None], wkv_p, k_p.reshape(kv_shape_p), v_p.reshape(kv_shape_p),
            x_sample[:, -1][None], wkv_s, k_s.reshape(kv_shape_s), v_s.reshape(kv_shape_s))
```

```python
import functools
import math

import jax
import jax.numpy as jnp
from jax import lax
from jax.experimental import pallas as pl
from jax.experimental.pallas import tpu as pltpu

F32 = jnp.float32
BF16 = jnp.bfloat16

D_MODEL = 1024
DEPTH = 2
RW_HEAD = 64
RW_HEADS = D_MODEL // RW_HEAD
RW_GN_EPS = 64e-5
DA_HEAD = 64
DA_HEADS = D_MODEL // (2 * DA_HEAD)
DA_VDIM = 2 * DA_HEAD
RMS_EPS = 1e-5
NEG_INF = -1e30
PEER_HEADS = 8
PEER_NKEYS = 128
PEER_NEXPERTS = PEER_NKEYS * PEER_NKEYS
PEER_QDIM = 256
PEER_HALF = PEER_QDIM // 2
PEER_TOPK = 16
DN_ALPHA = (2.0 * DEPTH) ** 0.25
LN_EPS = 1e-5

LANES = 128
BF16_ROWS = 16
VMEM_LIMIT = 56 * 1024 * 1024

ROW_TILE = 256
ROUTE_TILE = 256
PEER_TOK_TILE = 512
PEER_TOK_SPLIT = 1
PEER_EXP_TILE = 2048
SCAN_TCHUNK = 32
SCAN_VROWS = 4
SCAN_UNROLL = 8
ATT_TILE = 512
SAMPLE_PAGES = 16


def _cparams(n_axes):
    return pltpu.CompilerParams(dimension_semantics=("arbitrary",) * n_axes,
                                vmem_limit_bytes=VMEM_LIMIT)


def _dot(a, b):
    return jnp.dot(a, b, preferred_element_type=F32)


def _dot_nt(a, b):
    return lax.dot_general(a, b, (((1,), (1,)), ((), ())), preferred_element_type=F32)


def _layer_norm(x, g, b):
    mu = jnp.mean(x, axis=-1, keepdims=True)
    xc = x - mu
    var = jnp.mean(xc * xc, axis=-1, keepdims=True)
    return xc * lax.rsqrt(var + LN_EPS) * g + b


def _split2(x):
    hi = x.astype(BF16)
    lo = (x - hi.astype(F32)).astype(BF16)
    return hi, lo


def _head_sum(x, ind, ind_t):
    hi, lo = _split2(x)
    s = _dot(hi, ind) + _dot(lo, ind)
    shi, slo = _split2(s)
    return _dot(shi, ind_t) + _dot(slo, ind_t)


def _full(shape):
    n = len(shape)
    return pl.BlockSpec(shape, lambda *_: (0,) * n)


def _rwkv_pre_kernel(x_ref, xp_ref, mu_ref, wr_ref, wk_ref, wv_ref, w1_ref, w2_ref, a1_ref, a2_ref,
                     g1_ref, g2_ref, vec_ref, ind_ref, indt_ref,
                     r_ref, w_ref, k_ref, v_ref, a_ref, b_ref, g_ref):
    x = x_ref[...]
    xx = xp_ref[...] - x
    mix = lambda i: (x + xx * mu_ref[i:i + 1, :]).astype(BF16)
    xr, xw, xk, xv, xa, xg = (mix(i) for i in range(6))
    w0, a0, k_k, k_a = (vec_ref[i:i + 1, :] for i in range(4))
    r = _dot(xr, wr_ref[...])
    k = _dot(xk, wk_ref[...])
    v = _dot(xv, wv_ref[...])
    wl = w0 + _dot(jnp.tanh(_dot(xw, w1_ref[...])).astype(BF16), w2_ref[...])
    z = -wl
    softplus = jnp.maximum(z, 0.0) + jnp.log(1.0 + jnp.exp(-jnp.abs(z)))
    decay = jnp.exp(-jnp.exp(-softplus - 0.5))
    a = jax.nn.sigmoid(a0 + _dot(_dot(xa, a1_ref[...]).astype(BF16), a2_ref[...]))
    g = _dot(jax.nn.sigmoid(_dot(xg, g1_ref[...])).astype(BF16), g2_ref[...])
    kk = k * k_k
    n2 = _head_sum(kk * kk, ind_ref[...], indt_ref[...])
    kk = kk / jnp.maximum(jnp.sqrt(n2), 1e-12)
    r_ref[...] = r
    w_ref[...] = decay
    k_ref[...] = k * (1.0 + (a - 1.0) * k_a)
    v_ref[...] = v
    a_ref[...] = -kk
    b_ref[...] = kk * a
    g_ref[...] = g


def _rwkv_pre(x, xp, mu, w_r, w_k, w_v, w1, w2, a1, a2, g1, g2, vec, ind, ind_t):
    n = x.shape[0]
    tm = ROW_TILE
    row = pl.BlockSpec((tm, D_MODEL), lambda i: (i, 0))
    consts = (mu, w_r, w_k, w_v, w1, w2, a1, a2, g1, g2, vec, ind, ind_t)
    return pl.pallas_call(
        _rwkv_pre_kernel,
        grid=(n // tm,),
        in_specs=[row, row] + [_full(c.shape) for c in consts],
        out_specs=[row] * 7,
        out_shape=[jax.ShapeDtypeStruct((n, D_MODEL), F32)] * 7,
        compiler_params=_cparams(1),
    )(x, xp, *consts)


def _scan_kernel(r_ref, w_ref, k_ref, v_ref, a_ref, b_ref, s0_ref, y_ref, st_ref, wr_ref, dot_ref,
                 *, tchunk):
    @pl.when(pl.program_id(1) == 0)
    def _():
        st_ref[...] = s0_ref[...]

    def prep(t, carry):
        r = r_ref[0, t]
        wr_ref[t] = w_ref[0, t] * r
        dot_ref[t, 0:1, :] = jnp.sum(b_ref[0, t] * r, axis=0, keepdims=True)
        dot_ref[t, 1:2, :] = jnp.sum(k_ref[0, t] * r, axis=0, keepdims=True)
        return carry

    lax.fori_loop(0, tchunk, prep, 0)

    def vloop(vc, carry):
        v0 = vc * SCAN_VROWS
        state = tuple(st_ref[0, v0 + j] for j in range(SCAN_VROWS))

        def step(t, st):
            a = a_ref[0, t]
            wr = wr_ref[t]
            w = w_ref[0, t]
            b = b_ref[0, t]
            k = k_ref[0, t]
            br = dot_ref[t, 0:1, :]
            kr = dot_ref[t, 1:2, :]
            out = []
            for j in range(SCAN_VROWS):
                s = st[j]
                val = v_ref[0, t, pl.ds(v0 + j, 1), :]
                sa = jnp.sum(s * a, axis=0, keepdims=True)
                y0 = jnp.sum(s * wr, axis=0, keepdims=True)
                y_ref[0, t, pl.ds(v0 + j, 1), :] = y0 + sa * br + val * kr
                out.append(s * w + sa * b + val * k)
            return tuple(out)

        state = lax.fori_loop(0, tchunk, step, state, unroll=SCAN_UNROLL)
        for j in range(SCAN_VROWS):
            st_ref[0, v0 + j] = state[j]
        return carry

    lax.fori_loop(0, RW_HEAD // SCAN_VROWS, vloop, 0)


def _wkv_scan(r, w, k, v, a, b, s0):
    g, t = r.shape[0], r.shape[1]
    tchunk = min(SCAN_TCHUNK, t)
    seq = pl.BlockSpec((1, tchunk, RW_HEAD, LANES), lambda gi, ti: (gi, ti, 0, 0))
    st = pl.BlockSpec((1, RW_HEAD, RW_HEAD, LANES), lambda gi, ti: (gi, 0, 0, 0))
    return pl.pallas_call(
        functools.partial(_scan_kernel, tchunk=tchunk),
        grid=(g, t // tchunk),
        in_specs=[seq] * 6 + [st],
        out_specs=[seq, st],
        out_shape=[jax.ShapeDtypeStruct(r.shape, F32), jax.ShapeDtypeStruct(s0.shape, F32)],
        scratch_shapes=[pltpu.VMEM((tchunk, RW_HEAD, LANES), F32), pltpu.VMEM((tchunk, 8, LANES), F32)],
        compiler_params=_cparams(2),
    )(r, w, k, v, a, b, s0)


def _rwkv_post_kernel(x_ref, y_ref, r_ref, k_ref, v_ref, g_ref, vec_ref, ind_ref, indt_ref, wo_ref,
                      o_ref):
    ind, ind_t = ind_ref[...], indt_ref[...]
    gn_g, gn_b, r_k, ln_g, ln_b = (vec_ref[i:i + 1, :] for i in range(5))
    y = y_ref[...]
    inv = 1.0 / RW_HEAD
    yc = y - _head_sum(y, ind, ind_t) * inv
    var = _head_sum(yc * yc, ind, ind_t) * inv
    yn = yc * lax.rsqrt(var + RW_GN_EPS) * gn_g + gn_b
    v = v_ref[...]
    bonus = _head_sum(r_ref[...] * k_ref[...] * r_k, ind, ind_t) * v
    h = _dot(((yn + bonus) * g_ref[...]).astype(BF16), wo_ref[...])
    o_ref[...] = _layer_norm(DN_ALPHA * x_ref[...] + h, ln_g, ln_b)


def _rwkv_post(x, y, r, k, v, g, vec, ind, ind_t, w_o):
    n = x.shape[0]
    tm = ROW_TILE
    row = pl.BlockSpec((tm, D_MODEL), lambda i: (i, 0))
    consts = (vec, ind, ind_t, w_o)
    return pl.pallas_call(
        _rwkv_post_kernel,
        grid=(n // tm,),
        in_specs=[row] * 6 + [_full(c.shape) for c in consts],
        out_specs=row,
        out_shape=jax.ShapeDtypeStruct((n, D_MODEL), F32),
        compiler_params=_cparams(1),
    )(x, y, r, k, v, g, *consts)


def _proj_ln_kernel(x_ref, h_ref, vec_ref, wo_ref, o_ref):
    h = _dot(h_ref[...].astype(BF16), wo_ref[...])
    o_ref[...] = _layer_norm(DN_ALPHA * x_ref[...] + h, vec_ref[0:1, :], vec_ref[1:2, :])


def _proj_ln(x, h, vec, w_o):
    n = x.shape[0]
    tm = ROW_TILE
    row = pl.BlockSpec((tm, D_MODEL), lambda i: (i, 0))
    return pl.pallas_call(
        _proj_ln_kernel,
        grid=(n // tm,),
        in_specs=[row, row, _full(vec.shape), _full(w_o.shape)],
        out_specs=row,
        out_shape=jax.ShapeDtypeStruct((n, D_MODEL), F32),
        compiler_params=_cparams(1),
    )(x, h, vec, w_o)


SUBLANES = 8


def _sort_network(n):
    def merge(lo, hi, r):
        step = r * 2
        if step < hi - lo:
            yield from merge(lo, hi, step)
            yield from merge(lo + r, hi, step)
            yield from [(i, i + r) for i in range(lo + r, hi - r, step)]
        else:
            yield (lo, lo + r)

    def sort(lo, hi):
        if hi - lo >= 1:
            mid = lo + (hi - lo) // 2
            yield from sort(lo, mid)
            yield from sort(mid + 1, hi)
            yield from merge(lo, hi, 1)

    return list(sort(0, n - 1))


def _bf16_pair_words(x):
    bits = lax.bitcast_convert_type(x.astype(BF16).astype(F32), jnp.int32)
    return bits | lax.shift_right_logical(bits, jnp.full_like(bits, 16))


def _top_list(s, top_ref, cnt_ref, c):
    stack = [s[g * SUBLANES:(g + 1) * SUBLANES, :] for g in range(s.shape[0] // SUBLANES)]
    for i, j in _sort_network(len(stack)):
        stack[i], stack[j] = jnp.maximum(stack[i], stack[j]), jnp.minimum(stack[i], stack[j])
    for j in range(PEER_TOPK):
        m = jnp.max(stack[0], axis=0, keepdims=True)
        hit = stack[0] >= m
        top_ref[c, j:j + 1, :] = m
        cnt_ref[c, j:j + 1, :] = jnp.sum(jnp.where(hit, 1.0, 0.0), axis=0, keepdims=True)
        for r in range(PEER_TOPK - 1 - j):
            stack[r] = jnp.where(hit, stack[r + 1], stack[r])


def _peer_route_kernel(x_ref, wq_ref, bq_ref, sk_ref, xt_ref, rk_ref, e2_ref, e1_ref, ns_ref,
                       top_ref, cnt_ref):
    xt = x_ref[...].T.astype(BF16)
    xt_ref[...] = xt
    q = _dot(wq_ref[...], xt) + bq_ref[...]
    k = PEER_TOPK
    assert PEER_NKEYS // SUBLANES >= k and k == 2 * SUBLANES
    for h in range(PEER_HEADS):
        sc = []
        for c in range(2):
            lo = (2 * h + c) * PEER_HALF
            s = _dot(sk_ref[c], q[lo:lo + PEER_HALF, :].astype(BF16))
            sc.append(s)
            _top_list(s, top_ref, cnt_ref, c)
        ta, tb = top_ref[0], top_ref[1]
        ca, cb = cnt_ref[0], cnt_ref[1]
        lo_a, hi_a = ta[:SUBLANES], ta[SUBLANES:]
        cand = [lo_a + tb[j:j + 1, :] for j in range(k)]
        mult = [ca[:SUBLANES] * cb[j:j + 1, :] for j in range(k)]
        cand_hi = hi_a + tb[0:1, :]
        mult_hi = ca[SUBLANES:] * cb[0:1, :]
        best = thr = zsum = None
        remaining = jnp.full((1, xt.shape[1]), float(k), F32)
        for j in range(k):
            m = jnp.max(jnp.maximum(cand[0], cand_hi), axis=0, keepdims=True)
            hit, hit_hi = cand[0] >= m, cand_hi >= m
            n_hit = jnp.where(hit, mult[0], 0.0) + jnp.where(hit_hi, mult_hi, 0.0)
            take = jnp.minimum(jnp.sum(n_hit, axis=0, keepdims=True), remaining)
            if j == 0:
                best, thr, zsum = m, m, take
            else:
                zsum = zsum + take * jnp.exp(m - best)
                thr = jnp.where(take > 0.0, m, thr)
            remaining = remaining - take
            for r in range(k - 1 - j):
                cand[r] = jnp.where(hit, cand[r + 1], cand[r])
                mult[r] = jnp.where(hit, mult[r + 1], mult[r])
            cand_hi = jnp.where(hit_hi, -jnp.inf, cand_hi)
        s1, s2 = sc
        n_slot = jnp.zeros_like(ta)
        for j in range(k):
            n_slot = n_slot + jnp.where(ta + tb[j:j + 1, :] >= thr, 1.0, 0.0)
        nsel = jnp.zeros_like(s1)
        rank = jnp.full_like(s2, float(k))
        for j in reversed(range(k)):
            nsel = jnp.where(s1 >= ta[j:j + 1, :], n_slot[j:j + 1, :], nsel)
            rank = jnp.where(s2 >= tb[j:j + 1, :], float(j), rank)
        rk_ref[h] = rank.astype(BF16)
        ns_ref[h] = _bf16_pair_words(nsel)
        e1_ref[h] = _bf16_pair_words(jnp.exp(s1 - ta[0:1, :]))
        e2_ref[h] = (jnp.exp(s2 - tb[0:1, :]) / zsum).astype(BF16)


def _peer_route(x, wq_t, bq_b, subkeys):
    n = x.shape[0]
    tb = ROUTE_TILE
    tab = pl.BlockSpec((PEER_HEADS, PEER_NKEYS, tb), lambda i: (0, 0, i))
    tab_words = jax.ShapeDtypeStruct((PEER_HEADS, PEER_NKEYS, n), jnp.int32)
    tab_bf16 = jax.ShapeDtypeStruct((PEER_HEADS, PEER_NKEYS, n), BF16)
    return pl.pallas_call(
        _peer_route_kernel,
        grid=(n // tb,),
        in_specs=[pl.BlockSpec((tb, D_MODEL), lambda i: (i, 0)),
                  _full(wq_t.shape), _full(bq_b.shape), _full(subkeys.shape)],
        out_specs=[pl.BlockSpec((D_MODEL, tb), lambda i: (0, i)), tab, tab, tab, tab],
        out_shape=[jax.ShapeDtypeStruct((D_MODEL, n), BF16), tab_bf16, tab_bf16, tab_words, tab_words],
        scratch_shapes=[pltpu.VMEM((2, PEER_TOPK, tb), F32), pltpu.VMEM((2, PEER_TOPK, tb), F32)],
        compiler_params=_cparams(1),
    )(x, wq_t, bq_b, subkeys)


def _gelu_tanh(x):
    c0 = math.sqrt(2.0 / math.pi)
    inner = x * (jnp.asarray(c0, x.dtype) + jnp.asarray(c0 * 0.044715, x.dtype) * (x * x))
    hx = jnp.asarray(0.5, x.dtype) * x
    return hx + hx * jnp.tanh(inner)


def _peer_sweep_kernel(x_ref, xt_ref, rk_ref, e2_ref, e1_ref, ns_ref, u_ref, vt_ref, vec_ref,
                       o_ref, acc_ref):
    e = pl.program_id(1)

    @pl.when(e == 0)
    def _():
        acc_ref[...] = jnp.zeros_like(acc_ref)

    n_sub = PEER_EXP_TILE // PEER_NKEYS
    tb = xt_ref.shape[1]
    zero = jnp.zeros((), BF16)
    first = pl.multiple_of(e * n_sub, n_sub)

    def row_tile(ref, h, j, lanes):
        row = ref[h, pl.ds(first, n_sub), lanes][j:j + 1, :]
        tile = pltpu.bitcast(jnp.broadcast_to(row, (BF16_ROWS // 2, row.shape[1])), BF16)
        return jnp.concatenate([tile] * (PEER_NKEYS // BF16_ROWS), axis=0)

    sub = tb // PEER_TOK_SPLIT
    for part in range(PEER_TOK_SPLIT):
        lanes = slice(part * sub, (part + 1) * sub)
        xt = xt_ref[:, lanes]
        acts = []
        for j in range(n_sub):
            rows = slice(j * PEER_NKEYS, (j + 1) * PEER_NKEYS)
            gate = None
            for h in range(PEER_HEADS):
                keep = rk_ref[h, :, lanes] < row_tile(ns_ref, h, j, lanes)
                term = jnp.where(keep, e2_ref[h, :, lanes], zero) * row_tile(e1_ref, h, j, lanes)
                gate = term if gate is None else gate + term
            ht = _dot(u_ref[rows, :], xt)
            acts.append(gate * _gelu_tanh(ht.astype(BF16)))
        acc_ref[:, lanes] += _dot(vt_ref[...], jnp.concatenate(acts, axis=0))

    @pl.when(e == pl.num_programs(1) - 1)
    def _():
        c = acc_ref[...].T
        o_ref[...] = _layer_norm(DN_ALPHA * x_ref[...] + c, vec_ref[0:1, :], vec_ref[1:2, :])


def _peer_sweep(x, xt, rk, e2, e1, nsel, u_bf, vt_bf, vec):
    n = x.shape[0]
    tb, eb = PEER_TOK_TILE, PEER_EXP_TILE
    tab = pl.BlockSpec((PEER_HEADS, PEER_NKEYS, tb), lambda i, e: (0, 0, i))
    return pl.pallas_call(
        _peer_sweep_kernel,
        grid=(n // tb, PEER_NEXPERTS // eb),
        in_specs=[pl.BlockSpec((tb, D_MODEL), lambda i, e: (i, 0)),
                  pl.BlockSpec((D_MODEL, tb), lambda i, e: (0, i)),
                  tab, tab, tab, tab,
                  pl.BlockSpec((eb, D_MODEL), lambda i, e: (e, 0)),
                  pl.BlockSpec((D_MODEL, eb), lambda i, e: (0, e)),
                  pl.BlockSpec(vec.shape, lambda i, e: (0, 0))],
        out_specs=pl.BlockSpec((tb, D_MODEL), lambda i, e: (i, 0)),
        out_shape=jax.ShapeDtypeStruct((n, D_MODEL), F32),
        scratch_shapes=[pltpu.VMEM((D_MODEL, tb), F32)],
        compiler_params=_cparams(2),
    )(x, xt, rk, e2, e1, nsel, u_bf, vt_bf, vec)


def _ple_kernel(x_ref, p_ref, wp_ref, wg_ref, bg_ref, *rest, n_extra):
    we_refs, o_ref, e_refs = rest[:n_extra], rest[n_extra], rest[n_extra + 1:]
    x = x_ref[...]
    gate = jax.nn.sigmoid(_dot(x.astype(BF16), wg_ref[...]) + bg_ref[...])
    x3 = x + _dot(p_ref[...].astype(BF16), wp_ref[...]) * gate
    o_ref[...] = x3
    x3b = x3.astype(BF16)
    for we_ref, e_ref in zip(we_refs, e_refs):
        e_ref[...] = _dot(x3b, we_ref[...])


def _ple(x, p, w_p, w_g, b_g, w_extra=()):
    n = x.shape[0]
    tm = ROW_TILE
    row = pl.BlockSpec((tm, D_MODEL), lambda i: (i, 0))
    n_extra = len(w_extra)
    return pl.pallas_call(
        functools.partial(_ple_kernel, n_extra=n_extra),
        grid=(n // tm,),
        in_specs=[row, pl.BlockSpec((tm, p.shape[1]), lambda i: (i, 0)),
                  _full(w_p.shape), _full(w_g.shape), _full(b_g.shape)]
                 + [_full(w.shape) for w in w_extra],
        out_specs=[row] * (1 + n_extra),
        out_shape=[jax.ShapeDtypeStruct((n, D_MODEL), F32)] * (1 + n_extra),
        compiler_params=_cparams(1),
    )(x, p, w_p, w_g, b_g, *w_extra)


def _subln(of, g, lam_init):
    of = of * lax.rsqrt(jnp.mean(of * of, axis=-1, keepdims=True) + RMS_EPS) * g
    return of * (1.0 - lam_init)


def _attn_prompt_kernel(scal_ref, q_ref, k_ref, v_ref, g_ref, o_ref, kb_ref, vt_ref, bias_ref,
                        m_ref, l_ref, acc_ref, *, lam_init):
    h = pl.program_id(1)
    qi = pl.program_id(2)
    tq = ATT_TILE
    slope = scal_ref[h]
    lam = scal_ref[DA_HEADS]
    d0 = (lax.broadcasted_iota(jnp.int32, (tq, 2 * tq), 1) % tq
          - lax.broadcasted_iota(jnp.int32, (tq, 2 * tq), 0))

    @pl.when(qi == 0)
    def _():
        kb_ref[...] = k_ref[...].astype(BF16)
        vt_ref[...] = v_ref[...].T.astype(BF16)
        bias_ref[...] = slope * d0.astype(F32)

    qt = (q_ref[...] * (DA_HEAD ** -0.5)).T
    sub = lax.broadcasted_iota(jnp.int32, qt.shape, 0)
    qs = jnp.concatenate([jnp.where(sub < DA_HEAD, qt, 0.0), jnp.where(sub >= DA_HEAD, qt, 0.0)],
                         axis=1).astype(BF16)
    m_ref[...] = jnp.full_like(m_ref, -jnp.inf)
    l_ref[...] = jnp.zeros_like(l_ref)
    acc_ref[...] = jnp.zeros_like(acc_ref)

    def block(ki, bias, shift):
        off = pl.multiple_of(ki * tq, tq)
        s = _dot(kb_ref[pl.ds(off, tq), :], qs) - bias
        m_prev = m_ref[...]
        m_new = jnp.maximum(m_prev, jnp.max(s, axis=0, keepdims=True) - shift)
        alpha = jnp.exp(m_prev - m_new)
        p = jnp.exp(s - (m_new + shift))
        l_ref[...] = alpha * l_ref[...] + jnp.sum(p, axis=0, keepdims=True)
        acc_ref[...] = alpha * acc_ref[...] + _dot(vt_ref[:, pl.ds(off, tq)], p.astype(BF16))
        m_ref[...] = m_new

    def past(ki, carry):
        block(ki, bias_ref[...], slope * ((qi - ki) * tq).astype(F32))
        return carry

    lax.fori_loop(0, qi, past, 0)
    block(qi, jnp.where(d0 >= 0, bias_ref[...], -NEG_INF), 0.0)
    o = acc_ref[...] / l_ref[...]
    of = (o[:, :tq] - lam * o[:, tq:]).T
    o_ref[...] = _subln(of, g_ref[...], lam_init)


def _attn_prompt(scal, q, k, v, g, batch, seq, lam_init):
    tq = ATT_TILE
    nq = seq // tq
    grid_spec = pltpu.PrefetchScalarGridSpec(
        num_scalar_prefetch=1,
        grid=(batch, DA_HEADS, nq),
        in_specs=[pl.BlockSpec((tq, DA_VDIM), lambda b, h, i, s: (b * nq + i, h)),
                  pl.BlockSpec((seq, DA_VDIM), lambda b, h, i, s: (b, h)),
                  pl.BlockSpec((seq, DA_VDIM), lambda b, h, i, s: (b, h)),
                  pl.BlockSpec((1, DA_VDIM), lambda b, h, i, s: (0, 0))],
        out_specs=pl.BlockSpec((tq, DA_VDIM), lambda b, h, i, s: (b * nq + i, h)),
        scratch_shapes=[pltpu.VMEM((seq, DA_VDIM), BF16), pltpu.VMEM((DA_VDIM, seq), BF16),
                        pltpu.VMEM((tq, 2 * tq), F32),
                        pltpu.VMEM((1, 2 * tq), F32), pltpu.VMEM((1, 2 * tq), F32),
                        pltpu.VMEM((DA_VDIM, 2 * tq), F32)])
    return pl.pallas_call(
        functools.partial(_attn_prompt_kernel, lam_init=lam_init),
        grid_spec=grid_spec,
        out_shape=jax.ShapeDtypeStruct((batch * seq, D_MODEL), F32),
        compiler_params=_cparams(3),
    )(scal, q, k, v, g)


NEW_TOKENS_PAD = LANES // DA_HEADS


def _attn_sample_kernel(pt_ref, scal_ref, q_ref, *rest, dec_seq, page, lam_init):
    k_refs, v_refs = rest[:SAMPLE_PAGES], rest[SAMPLE_PAGES:2 * SAMPLE_PAGES]
    (kn_ref, vn_ref, bias_ref, biasn_ref, g_ref, o_ref, qs_ref, m_ref, l_ref, acc_ref
     ) = rest[2 * SAMPLE_PAGES:]
    step = pl.program_id(1)
    rows = 2 * dec_seq * DA_HEADS
    row_h = lax.broadcasted_iota(jnp.int32, (rows, 1), 0) % DA_HEADS
    slope = jnp.zeros((rows, 1), F32)
    for h in range(DA_HEADS):
        slope = jnp.where(row_h == h, scal_ref[h], slope)

    @pl.when(step == 0)
    def _():
        q = q_ref[0] * (DA_HEAD ** -0.5)
        lane = lax.broadcasted_iota(jnp.int32, q.shape, 1)
        qs_ref[...] = jnp.concatenate(
            [jnp.where(lane < DA_HEAD, q, 0.0), jnp.where(lane >= DA_HEAD, q, 0.0)],
            axis=0).astype(BF16)
        m_ref[...] = jnp.full_like(m_ref, -jnp.inf)
        l_ref[...] = jnp.zeros_like(l_ref)
        acc_ref[...] = jnp.zeros_like(acc_ref)

    def update(scores, values):
        m_prev = m_ref[...]
        m_new = functools.reduce(jnp.maximum,
                                 [jnp.max(s, axis=-1, keepdims=True) for s in scores], m_prev)
        alpha = jnp.exp(m_prev - m_new)
        l_new = alpha * l_ref[...]
        acc = alpha * acc_ref[...]
        for s, vb in zip(scores, values):
            pr = jnp.exp(s - m_new)
            l_new = l_new + jnp.sum(pr, axis=-1, keepdims=True)
            acc = acc + _dot(pr.astype(BF16), vb)
        m_ref[...] = m_new
        l_ref[...] = l_new
        acc_ref[...] = acc

    flat = lambda ref: ref[0].reshape(page * DA_HEADS, DA_VDIM).astype(BF16)
    scores = []
    for j in range(SAMPLE_PAGES):
        first_tok = ((step * SAMPLE_PAGES + j) * page).astype(F32)
        scores.append(_dot_nt(qs_ref[...], flat(k_refs[j])) - bias_ref[...] + slope * first_tok)
    update(scores, [flat(v_ref) for v_ref in v_refs])

    @pl.when(step == pl.num_programs(1) - 1)
    def _():
        update([_dot_nt(qs_ref[...], kn_ref[0].astype(BF16)) - biasn_ref[...]],
               [vn_ref[0].astype(BF16)])
        o = acc_ref[...] / l_ref[...]
        of = o[:rows // 2] - scal_ref[DA_HEADS] * o[rows // 2:]
        o_ref[0] = _subln(of, g_ref[...], lam_init)


def _attn_sample(page_table, scal, q, cache_k, cache_v, k_new, v_new, g, lam_init):
    b = q.shape[0]
    dec_seq = q.shape[1] // DA_HEADS
    n_pages = page_table.shape[1]
    page = cache_k.shape[1]
    past_len = n_pages * page
    rows = 2 * dec_seq * DA_HEADS
    assert n_pages % SAMPLE_PAGES == 0 and dec_seq <= NEW_TOKENS_PAD

    slopes = scal[:DA_HEADS]
    r = jnp.arange(rows)
    r_q, r_h = (r // DA_HEADS) % dec_seq, r % DA_HEADS

    def bias_tile(n_tok, first_q_pos, causal):
        col = jnp.arange(n_tok * DA_HEADS)
        dist = first_q_pos + r_q[:, None] - (col // DA_HEADS)[None, :]
        ok = (col % DA_HEADS)[None, :] == r_h[:, None]
        if causal:
            ok = ok & (dist >= 0)
        return jnp.where(ok, slopes[r_h][:, None] * dist.astype(F32), -NEG_INF)

    bias_past = bias_tile(page, past_len, False)
    bias_new = bias_tile(NEW_TOKENS_PAD, 0, True)

    def page_spec(j):
        return pl.BlockSpec((1, page, DA_HEADS, DA_VDIM),
                            lambda i, p, pt, s: (pt[i * n_pages + p * SAMPLE_PAGES + j], 0, 0, 0))

    per_seq = lambda shape: pl.BlockSpec((1,) + shape, lambda i, p, pt, s: (i, 0, 0))
    const = lambda shape: pl.BlockSpec(shape, lambda i, p, pt, s: (0, 0))
    grid_spec = pltpu.PrefetchScalarGridSpec(
        num_scalar_prefetch=2,
        grid=(b, n_pages // SAMPLE_PAGES),
        in_specs=[per_seq((dec_seq * DA_HEADS, DA_VDIM))]
                 + [page_spec(j) for j in range(SAMPLE_PAGES)] * 2
                 + [per_seq((LANES, DA_VDIM)), per_seq((LANES, DA_VDIM)),
                    const(bias_past.shape), const(bias_new.shape), const((1, DA_VDIM))],
        out_specs=per_seq((dec_seq * DA_HEADS, DA_VDIM)),
        scratch_shapes=[pltpu.VMEM((rows, DA_VDIM), BF16),
                        pltpu.VMEM((rows, 1), F32), pltpu.VMEM((rows, 1), F32),
                        pltpu.VMEM((rows, DA_VDIM), F32)])
    return pl.pallas_call(
        functools.partial(_attn_sample_kernel, dec_seq=dec_seq, page=page, lam_init=lam_init),
        grid_spec=grid_spec,
        out_shape=jax.ShapeDtypeStruct(q.shape, F32),
        compiler_params=_cparams(2),
    )(page_table.reshape(-1), scal, q, *([cache_k] * SAMPLE_PAGES), *([cache_v] * SAMPLE_PAGES),
      k_new, v_new, bias_past, bias_new, g)


def _lambda_init(layer):
    return 0.8 - 0.6 * math.exp(-0.3 * layer)


def _peer_weights(layer, peer_w_q, peer_b_q, peer_subkeys, peer_u, peer_v, ln2_g, ln2_b):
    wq_t = peer_w_q[layer].T.astype(BF16)
    bq_b = jnp.broadcast_to(peer_b_q[layer][:, None], (PEER_HEADS * PEER_QDIM, ROUTE_TILE))
    return (wq_t, bq_b, peer_subkeys[layer].astype(BF16), peer_u[layer].astype(BF16),
            peer_v[layer].T.astype(BF16), jnp.stack([ln2_g[layer], ln2_b[layer]]))


def _peer_layer(x, wq_t, bq_b, subkeys, u_bf, vt_bf, ln_vec):
    xt, rk, e2, e1, nsel = _peer_route(x, wq_t, bq_b, subkeys)
    return _peer_sweep(x, xt, rk, e2, e1, nsel, u_bf, vt_bf, ln_vec)


def kernel(x_prompt, x_sample, state_shift, state_wkv, cache_k, cache_v, page_table, p_prompt, p_sample, rw_mu, rw_w_r, rw_w_k, rw_w_v, rw_w0, rw_w1, rw_w2, rw_a0, rw_a1, rw_a2, rw_g1, rw_g2, rw_k_k, rw_k_a, rw_r_k, rw_gn_g, rw_gn_b, rw_w_o, da_w_k, da_w_v, da_w_q, da_lam, da_subln_g, da_w_o, ln1_g, ln1_b, ln2_g, ln2_b, peer_w_q, peer_b_q, peer_subkeys, peer_u, peer_v, ple_w_p, ple_w_g, ple_b_g):
    bp, tp, d = x_prompt.shape
    bs, ts, _ = x_sample.shape
    assert d == D_MODEL and bp * RW_HEADS == LANES and bs == LANES
    np_, ns = bp * tp, bs * ts
    hh, hd = RW_HEADS, RW_HEAD
    bf = lambda w: w.astype(BF16)

    head_of_lane = jnp.arange(d) // hd
    ind = (head_of_lane[:, None] == jnp.arange(LANES)[None, :]).astype(BF16)
    ind_t = ind.T
    pre_consts = (rw_mu[0], bf(rw_w_r[0]), bf(rw_w_k[0]), bf(rw_w_v[0]), bf(rw_w1[0]), bf(rw_w2[0]),
                  bf(rw_a1[0]), bf(rw_a2[0]), bf(rw_g1[0]), bf(rw_g2[0]),
                  jnp.stack([rw_w0[0], rw_a0[0], rw_k_k[0], rw_k_a[0]]), ind, ind_t)
    post_consts = (jnp.stack([rw_gn_g[0], rw_gn_b[0], rw_r_k[0].reshape(d), ln1_g[0], ln1_b[0]]),
                   ind, ind_t, bf(rw_w_o[0]))
    peer_consts = [_peer_weights(layer, peer_w_q, peer_b_q, peer_subkeys, peer_u, peer_v, ln2_g, ln2_b)
                   for layer in range(DEPTH)]
    ple_consts = [(bf(ple_w_p[layer]), bf(ple_w_g[layer]), ple_b_g[layer][None])
                  for layer in range(DEPTH)]
    kvq_w = (bf(da_w_k), bf(da_w_v), bf(da_w_q[0]))
    attn_out = (jnp.stack([ln1_g[1], ln1_b[1]]), bf(da_w_o[0]))
    lam_init = _lambda_init(1)
    lv = da_lam[0].astype(F32)
    lam = jnp.exp(jnp.sum(lv[0] * lv[1])) - jnp.exp(jnp.sum(lv[2] * lv[3])) + lam_init
    slopes = 2.0 ** (-8.0 * jnp.arange(1, DA_HEADS + 1, dtype=F32) / DA_HEADS)
    scal = jnp.concatenate([slopes, lam[None]]).astype(F32)
    subln_g = da_subln_g[0][None]

    def trunk(x, x_prev, p, to_chains, from_chains, s0, attend):
        r, w, k, v, a, b, g = _rwkv_pre(x, x_prev, *pre_consts)
        y, state = _wkv_scan(*(to_chains(t) for t in (r, w, k, v, a, b)), s0)
        x = _rwkv_post(x, from_chains(y), r, k, v, g, *post_consts)
        x = _peer_layer(x, *peer_consts[0])
        x, k_sh, v_sh, q = _ple(x, p[0], *ple_consts[0], kvq_w)
        x = _proj_ln(x, attend(q, k_sh, v_sh), *attn_out)
        x = _peer_layer(x, *peer_consts[1])
        (x,) = _ple(x, p[1], *ple_consts[1])
        return x, state, k_sh, v_sh

    xp_prev = jnp.concatenate([jnp.zeros((bp, 1, d), F32), x_prompt[:, :-1]], axis=1)
    y_prompt, st_p, k_p, v_p = trunk(
        x_prompt.reshape(np_, d), xp_prev.reshape(np_, d), p_prompt.reshape(DEPTH, np_, -1),
        lambda t: t.reshape(bp, tp, hh, hd).transpose(1, 3, 0, 2).reshape(1, tp, hd, LANES),
        lambda y: y.reshape(tp, hd, bp, hh).transpose(2, 0, 3, 1).reshape(np_, d),
        jnp.zeros((1, hd, hd, LANES), F32),
        lambda q, k_sh, v_sh: _attn_prompt(scal, q, k_sh, v_sh, subln_g, bp, tp, lam_init))

    xs_prev = jnp.concatenate([state_shift[0][:, None, :], x_sample[:, :-1]], axis=1)
    new_rows = lambda t: jnp.pad(t.reshape(bs, ts, d), ((0, 0), (0, NEW_TOKENS_PAD - ts), (0, 0))
                                 ).reshape(bs, LANES, DA_VDIM)
    y_sample, st_s, k_s, v_s = trunk(
        x_sample.reshape(ns, d), xs_prev.reshape(ns, d), p_sample.reshape(DEPTH, ns, -1),
        lambda t: t.reshape(bs, ts, hh, hd).transpose(2, 1, 3, 0),
        lambda y: y.transpose(3, 1, 0, 2).reshape(ns, d),
        state_wkv[0].astype(F32).transpose(1, 2, 3, 0),
        lambda q, k_sh, v_sh: _attn_sample(
            page_table, scal, q.reshape(bs, ts * DA_HEADS, DA_VDIM), cache_k, cache_v,
            new_rows(k_sh), new_rows(v_sh), subln_g, lam_init).reshape(ns, d))

    wkv_p = st_p.reshape(hd, hd, bp, hh).transpose(2, 3, 0, 1)[None].astype(state_wkv.dtype)
    wkv_s = st_s.transpose(3, 0, 1, 2)[None].astype(state_wkv.dtype)
    kv_shape_p = (bp, tp, DA_HEADS, DA_VDIM)
    kv_shape_s = (bs, ts, DA_HEADS, DA_VDIM)
    return (y_prompt.reshape(bp, tp, d), y_sample.reshape(bs, ts, d),
            x_prompt[:, -1][None], wkv_p, k_p.reshape(kv_shape_p), v_p.reshape(kv_shape_p),
            x_sample[:, -1][None], wkv_s, k_s.reshape(kv_shape_s), v_s.reshape(kv_shape_s))
```

```python
import functools
import math

import jax
import jax.numpy as jnp
from jax import lax
from jax.experimental import pallas as pl
from jax.experimental.pallas import tpu as pltpu

F32 = jnp.float32
BF16 = jnp.bfloat16

D_MODEL = 1024
DEPTH = 2
RW_HEAD = 64
RW_HEADS = D_MODEL // RW_HEAD
RW_GN_EPS = 64e-5
DA_HEAD = 64
DA_HEADS = D_MODEL // (2 * DA_HEAD)
DA_VDIM = 2 * DA_HEAD
RMS_EPS = 1e-5
NEG_INF = -1e30
PEER_HEADS = 8
PEER_NKEYS = 128
PEER_NEXPERTS = PEER_NKEYS * PEER_NKEYS
PEER_QDIM = 256
PEER_HALF = PEER_QDIM // 2
PEER_TOPK = 16
DN_ALPHA = (2.0 * DEPTH) ** 0.25
LN_EPS = 1e-5

LANES = 128
BF16_ROWS = 16
VMEM_LIMIT = 56 * 1024 * 1024

ROW_TILE = 256
ROUTE_TILE = 256
PEER_TOK_TILE = 512
PEER_TOK_SPLIT = 1
PEER_EXP_TILE = 2048
SCAN_TCHUNK = 32
SCAN_VROWS = 4
SCAN_UNROLL = 8
ATT_TILE = 512
SAMPLE_PAGES = 16


def _cparams(n_axes):
    return pltpu.CompilerParams(dimension_semantics=("arbitrary",) * n_axes,
                                vmem_limit_bytes=VMEM_LIMIT)


def _dot(a, b):
    return jnp.dot(a, b, preferred_element_type=F32)


def _dot_nt(a, b):
    return lax.dot_general(a, b, (((1,), (1,)), ((), ())), preferred_element_type=F32)


def _layer_norm(x, g, b):
    mu = jnp.mean(x, axis=-1, keepdims=True)
    xc = x - mu
    var = jnp.mean(xc * xc, axis=-1, keepdims=True)
    return xc * lax.rsqrt(var + LN_EPS) * g + b


def _split2(x):
    hi = x.astype(BF16)
    lo = (x - hi.astype(F32)).astype(BF16)
    return hi, lo


def _head_sum(x, ind, ind_t):
    hi, lo = _split2(x)
    s = _dot(hi, ind) + _dot(lo, ind)
    shi, slo = _split2(s)
    return _dot(shi, ind_t) + _dot(slo, ind_t)


def _full(shape):
    n = len(shape)
    return pl.BlockSpec(shape, lambda *_: (0,) * n)


def _rwkv_pre_kernel(x_ref, xp_ref, *refs):
    _rwkv_pre_math(x_ref[...], xp_ref[...], *refs)


def _batch_to_time_major(ref, scratch):
    nb, steps, dm = ref.shape
    tiles = range(dm // LANES)
    for b in range(nb):
        blk = ref[b]
        for c in tiles:
            scratch[c, b * steps:(b + 1) * steps, :] = blk[:, c * LANES:(c + 1) * LANES]
    return jnp.concatenate(
        [jnp.concatenate([scratch[c, pl.ds(t, nb, stride=steps), :] for c in tiles], axis=1)
         for t in range(steps)], axis=0)


def _time_to_batch_major(val, ref, scratch):
    nb, steps, dm = ref.shape
    tiles = range(dm // LANES)
    for t in range(steps):
        for c in tiles:
            scratch[c, pl.ds(t, nb, stride=steps), :] = val[t * nb:(t + 1) * nb, c * LANES:(c + 1) * LANES]
    for b in range(nb):
        ref[b] = jnp.concatenate([scratch[c, b * steps:(b + 1) * steps, :] for c in tiles], axis=1)


def _rwkv_pre_tm_kernel(x_ref, xl_ref, shift_ref, *refs):
    *refs, flat_ref = refs
    nb = x_ref.shape[0]
    x = _batch_to_time_major(x_ref, flat_ref)
    last = jnp.concatenate([xl_ref[b, xl_ref.shape[1] - 1:, :] for b in range(nb)], axis=0)
    first = jnp.where(pl.program_id(0) == 0, shift_ref[...], last)
    _rwkv_pre_math(x, jnp.concatenate([first, x[:-nb]], axis=0), *refs)


def _rwkv_pre_math(x, xp, mu_ref, wr_ref, wk_ref, wv_ref, w1_ref, w2_ref, a1_ref, a2_ref,
                   g1_ref, g2_ref, vec_ref, ind_ref, indt_ref,
                   r_ref, w_ref, k_ref, v_ref, a_ref, b_ref, g_ref):
    xx = xp - x
    mix = lambda i: (x + xx * mu_ref[i:i + 1, :]).astype(BF16)
    xr, xw, xk, xv, xa, xg = (mix(i) for i in range(6))
    w0, a0, k_k, k_a = (vec_ref[i:i + 1, :] for i in range(4))
    r = _dot(xr, wr_ref[...])
    k = _dot(xk, wk_ref[...])
    v = _dot(xv, wv_ref[...])
    wl = w0 + _dot(jnp.tanh(_dot(xw, w1_ref[...])).astype(BF16), w2_ref[...])
    z = -wl
    softplus = jnp.maximum(z, 0.0) + jnp.log(1.0 + jnp.exp(-jnp.abs(z)))
    decay = jnp.exp(-jnp.exp(-softplus - 0.5))
    a = jax.nn.sigmoid(a0 + _dot(_dot(xa, a1_ref[...]).astype(BF16), a2_ref[...]))
    g = _dot(jax.nn.sigmoid(_dot(xg, g1_ref[...])).astype(BF16), g2_ref[...])
    kk = k * k_k
    n2 = _head_sum(kk * kk, ind_ref[...], indt_ref[...])
    kk = kk / jnp.maximum(jnp.sqrt(n2), 1e-12)
    r_ref[...] = r
    w_ref[...] = decay
    k_ref[...] = k * (1.0 + (a - 1.0) * k_a)
    v_ref[...] = v
    a_ref[...] = -kk
    b_ref[...] = kk * a
    g_ref[...] = g


def _rwkv_pre(x, xp, mu, w_r, w_k, w_v, w1, w2, a1, a2, g1, g2, vec, ind, ind_t):
    n = x.shape[0]
    tm = ROW_TILE
    row = pl.BlockSpec((tm, D_MODEL), lambda i: (i, 0))
    consts = (mu, w_r, w_k, w_v, w1, w2, a1, a2, g1, g2, vec, ind, ind_t)
    return pl.pallas_call(
        _rwkv_pre_kernel,
        grid=(n // tm,),
        in_specs=[row, row] + [_full(c.shape) for c in consts],
        out_specs=[row] * 7,
        out_shape=[jax.ShapeDtypeStruct((n, D_MODEL), F32)] * 7,
        compiler_params=_cparams(1),
    )(x, xp, *consts)


TM_LAST = 8


def _rwkv_pre_tm(x, shift, *consts):
    nb, t, _ = x.shape
    steps = ROW_TILE // nb
    row = pl.BlockSpec((ROW_TILE, D_MODEL), lambda i: (i, 0))
    return pl.pallas_call(
        _rwkv_pre_tm_kernel,
        grid=(t // steps,),
        in_specs=[pl.BlockSpec((nb, steps, D_MODEL), lambda i: (0, i, 0)),
                  pl.BlockSpec((nb, TM_LAST, D_MODEL),
                               lambda i: (0, jnp.maximum(i * (steps // TM_LAST) - 1, 0), 0)),
                  _full(shift.shape)] + [_full(c.shape) for c in consts],
        out_specs=[row] * 7,
        out_shape=[jax.ShapeDtypeStruct((t * nb, D_MODEL), F32)] * 7,
        scratch_shapes=[pltpu.VMEM((D_MODEL // LANES, ROW_TILE, LANES), F32)],
        compiler_params=_cparams(1),
    )(x, x, shift, *consts)


def _scan_kernel(r_ref, w_ref, k_ref, v_ref, a_ref, b_ref, s0_ref, y_ref, st_ref, wr_ref, dot_ref,
                 *, tchunk):
    @pl.when(pl.program_id(1) == 0)
    def _():
        st_ref[...] = s0_ref[...]

    def prep(t, carry):
        r = r_ref[0, t]
        wr_ref[t] = w_ref[0, t] * r
        dot_ref[t, 0:1, :] = jnp.sum(b_ref[0, t] * r, axis=0, keepdims=True)
        dot_ref[t, 1:2, :] = jnp.sum(k_ref[0, t] * r, axis=0, keepdims=True)
        return carry

    lax.fori_loop(0, tchunk, prep, 0)

    def vloop(vc, carry):
        v0 = vc * SCAN_VROWS
        state = tuple(st_ref[0, v0 + j] for j in range(SCAN_VROWS))

        def step(t, st):
            a = a_ref[0, t]
            wr = wr_ref[t]
            w = w_ref[0, t]
            b = b_ref[0, t]
            k = k_ref[0, t]
            br = dot_ref[t, 0:1, :]
            kr = dot_ref[t, 1:2, :]
            out = []
            for j in range(SCAN_VROWS):
                s = st[j]
                val = v_ref[0, t, pl.ds(v0 + j, 1), :]
                sa = jnp.sum(s * a, axis=0, keepdims=True)
                y0 = jnp.sum(s * wr, axis=0, keepdims=True)
                y_ref[0, t, pl.ds(v0 + j, 1), :] = y0 + sa * br + val * kr
                out.append(s * w + sa * b + val * k)
            return tuple(out)

        state = lax.fori_loop(0, tchunk, step, state, unroll=SCAN_UNROLL)
        for j in range(SCAN_VROWS):
            st_ref[0, v0 + j] = state[j]
        return carry

    lax.fori_loop(0, RW_HEAD // SCAN_VROWS, vloop, 0)


def _wkv_scan(r, w, k, v, a, b, s0):
    g, t = r.shape[0], r.shape[1]
    tchunk = min(SCAN_TCHUNK, t)
    seq = pl.BlockSpec((1, tchunk, RW_HEAD, LANES), lambda gi, ti: (gi, ti, 0, 0))
    st = pl.BlockSpec((1, RW_HEAD, RW_HEAD, LANES), lambda gi, ti: (gi, 0, 0, 0))
    return pl.pallas_call(
        functools.partial(_scan_kernel, tchunk=tchunk),
        grid=(g, t // tchunk),
        in_specs=[seq] * 6 + [st],
        out_specs=[seq, st],
        out_shape=[jax.ShapeDtypeStruct(r.shape, F32), jax.ShapeDtypeStruct(s0.shape, F32)],
        scratch_shapes=[pltpu.VMEM((tchunk, RW_HEAD, LANES), F32), pltpu.VMEM((tchunk, 8, LANES), F32)],
        compiler_params=_cparams(2),
    )(r, w, k, v, a, b, s0)


def _rwkv_post_math(x, y_ref, r_ref, k_ref, v_ref, g_ref, vec_ref, ind_ref, indt_ref, wo_ref):
    ind, ind_t = ind_ref[...], indt_ref[...]
    gn_g, gn_b, r_k, ln_g, ln_b = (vec_ref[i:i + 1, :] for i in range(5))
    y = y_ref[...]
    inv = 1.0 / RW_HEAD
    yc = y - _head_sum(y, ind, ind_t) * inv
    var = _head_sum(yc * yc, ind, ind_t) * inv
    yn = yc * lax.rsqrt(var + RW_GN_EPS) * gn_g + gn_b
    v = v_ref[...]
    bonus = _head_sum(r_ref[...] * k_ref[...] * r_k, ind, ind_t) * v
    h = _dot(((yn + bonus) * g_ref[...]).astype(BF16), wo_ref[...])
    return _layer_norm(DN_ALPHA * x + h, ln_g, ln_b)


def _rwkv_post_kernel(x_ref, *refs):
    *refs, o_ref = refs
    o_ref[...] = _rwkv_post_math(x_ref[...], *refs)


def _rwkv_post_tm_kernel(x_ref, *refs):
    *refs, o_ref, flat_ref = refs
    nb, steps, _ = x_ref.shape
    out = _rwkv_post_math(_batch_to_time_major(x_ref, flat_ref), *refs)
    _time_to_batch_major(out, o_ref, flat_ref)


def _rwkv_post_tm(x, y, r, k, v, g, *consts):
    nb, t, _ = x.shape
    steps = ROW_TILE // nb
    row = pl.BlockSpec((ROW_TILE, D_MODEL), lambda i: (i, 0))
    blk = pl.BlockSpec((nb, steps, D_MODEL), lambda i: (0, i, 0))
    return pl.pallas_call(
        _rwkv_post_tm_kernel,
        grid=(t // steps,),
        in_specs=[blk] + [row] * 5 + [_full(c.shape) for c in consts],
        out_specs=blk,
        out_shape=jax.ShapeDtypeStruct(x.shape, F32),
        scratch_shapes=[pltpu.VMEM((D_MODEL // LANES, ROW_TILE, LANES), F32)],
        compiler_params=_cparams(1),
    )(x, y, r, k, v, g, *consts)


def _rwkv_post(x, y, r, k, v, g, vec, ind, ind_t, w_o):
    n = x.shape[0]
    tm = ROW_TILE
    row = pl.BlockSpec((tm, D_MODEL), lambda i: (i, 0))
    consts = (vec, ind, ind_t, w_o)
    return pl.pallas_call(
        _rwkv_post_kernel,
        grid=(n // tm,),
        in_specs=[row] * 6 + [_full(c.shape) for c in consts],
        out_specs=row,
        out_shape=jax.ShapeDtypeStruct((n, D_MODEL), F32),
        compiler_params=_cparams(1),
    )(x, y, r, k, v, g, *consts)


def _proj_ln_kernel(x_ref, h_ref, vec_ref, wo_ref, o_ref):
    h = _dot(h_ref[...].astype(BF16), wo_ref[...])
    o_ref[...] = _layer_norm(DN_ALPHA * x_ref[...] + h, vec_ref[0:1, :], vec_ref[1:2, :])


def _proj_ln(x, h, vec, w_o):
    n = x.shape[0]
    tm = ROW_TILE
    row = pl.BlockSpec((tm, D_MODEL), lambda i: (i, 0))
    return pl.pallas_call(
        _proj_ln_kernel,
        grid=(n // tm,),
        in_specs=[row, row, _full(vec.shape), _full(w_o.shape)],
        out_specs=row,
        out_shape=jax.ShapeDtypeStruct((n, D_MODEL), F32),
        compiler_params=_cparams(1),
    )(x, h, vec, w_o)


SUBLANES = 8


def _sort_network(n):
    def merge(lo, hi, r):
        step = r * 2
        if step < hi - lo:
            yield from merge(lo, hi, step)
            yield from merge(lo + r, hi, step)
            yield from [(i, i + r) for i in range(lo + r, hi - r, step)]
        else:
            yield (lo, lo + r)

    def sort(lo, hi):
        if hi - lo >= 1:
            mid = lo + (hi - lo) // 2
            yield from sort(lo, mid)
            yield from sort(mid + 1, hi)
            yield from merge(lo, hi, 1)

    return list(sort(0, n - 1))


def _bf16_pair_words(x):
    bits = lax.bitcast_convert_type(x.astype(BF16).astype(F32), jnp.int32)
    return bits | lax.shift_right_logical(bits, jnp.full_like(bits, 16))


def _top_list(s, top_ref, cnt_ref, c):
    stack = [s[g * SUBLANES:(g + 1) * SUBLANES, :] for g in range(s.shape[0] // SUBLANES)]
    for i, j in _sort_network(len(stack)):
        stack[i], stack[j] = jnp.maximum(stack[i], stack[j]), jnp.minimum(stack[i], stack[j])
    for j in range(PEER_TOPK):
        m = jnp.max(stack[0], axis=0, keepdims=True)
        hit = stack[0] >= m
        top_ref[c, j:j + 1, :] = m
        cnt_ref[c, j:j + 1, :] = jnp.sum(jnp.where(hit, 1.0, 0.0), axis=0, keepdims=True)
        for r in range(PEER_TOPK - 1 - j):
            stack[r] = jnp.where(hit, stack[r + 1], stack[r])


def _peer_route_kernel(x_ref, wq_ref, bq_ref, sk_ref, xt_ref, rk_ref, e2_ref, e1_ref, ns_ref,
                       top_ref, cnt_ref):
    xt = x_ref[...].T.astype(BF16)
    xt_ref[...] = xt
    q = _dot(wq_ref[...], xt) + bq_ref[...]
    k = PEER_TOPK
    assert PEER_NKEYS // SUBLANES >= k and k == 2 * SUBLANES
    for h in range(PEER_HEADS):
        sc = []
        for c in range(2):
            lo = (2 * h + c) * PEER_HALF
            s = _dot(sk_ref[c], q[lo:lo + PEER_HALF, :].astype(BF16))
            sc.append(s)
            _top_list(s, top_ref, cnt_ref, c)
        ta, tb = top_ref[0], top_ref[1]
        ca, cb = cnt_ref[0], cnt_ref[1]
        lo_a, hi_a = ta[:SUBLANES], ta[SUBLANES:]
        cand = [lo_a + tb[j:j + 1, :] for j in range(k)]
        mult = [ca[:SUBLANES] * cb[j:j + 1, :] for j in range(k)]
        cand_hi = hi_a + tb[0:1, :]
        mult_hi = ca[SUBLANES:] * cb[0:1, :]
        best = thr = zsum = None
        remaining = jnp.full((1, xt.shape[1]), float(k), F32)
        for j in range(k):
            m = jnp.max(jnp.maximum(cand[0], cand_hi), axis=0, keepdims=True)
            hit, hit_hi = cand[0] >= m, cand_hi >= m
            n_hit = jnp.where(hit, mult[0], 0.0) + jnp.where(hit_hi, mult_hi, 0.0)
            take = jnp.minimum(jnp.sum(n_hit, axis=0, keepdims=True), remaining)
            if j == 0:
                best, thr, zsum = m, m, take
            else:
                zsum = zsum + take * jnp.exp(m - best)
                thr = jnp.where(take > 0.0, m, thr)
            remaining = remaining - take
            for r in range(k - 1 - j):
                cand[r] = jnp.where(hit, cand[r + 1], cand[r])
                mult[r] = jnp.where(hit, mult[r + 1], mult[r])
            cand_hi = jnp.where(hit_hi, -jnp.inf, cand_hi)
        s1, s2 = sc
        n_slot = jnp.zeros_like(ta)
        for j in range(k):
            n_slot = n_slot + jnp.where(ta + tb[j:j + 1, :] >= thr, 1.0, 0.0)
        nsel = jnp.zeros_like(s1)
        rank = jnp.full_like(s2, float(k))
        for j in reversed(range(k)):
            nsel = jnp.where(s1 >= ta[j:j + 1, :], n_slot[j:j + 1, :], nsel)
            rank = jnp.where(s2 >= tb[j:j + 1, :], float(j), rank)
        rk_ref[h] = rank.astype(BF16)
        ns_ref[h] = _bf16_pair_words(nsel)
        e1_ref[h] = _bf16_pair_words(jnp.exp(s1 - ta[0:1, :]))
        e2_ref[h] = (jnp.exp(s2 - tb[0:1, :]) / zsum).astype(BF16)


def _peer_route(x, wq_t, bq_b, subkeys):
    n = x.shape[0]
    tb = ROUTE_TILE
    tab = pl.BlockSpec((PEER_HEADS, PEER_NKEYS, tb), lambda i: (0, 0, i))
    tab_words = jax.ShapeDtypeStruct((PEER_HEADS, PEER_NKEYS, n), jnp.int32)
    tab_bf16 = jax.ShapeDtypeStruct((PEER_HEADS, PEER_NKEYS, n), BF16)
    return pl.pallas_call(
        _peer_route_kernel,
        grid=(n // tb,),
        in_specs=[pl.BlockSpec((tb, D_MODEL), lambda i: (i, 0)),
                  _full(wq_t.shape), _full(bq_b.shape), _full(subkeys.shape)],
        out_specs=[pl.BlockSpec((D_MODEL, tb), lambda i: (0, i)), tab, tab, tab, tab],
        out_shape=[jax.ShapeDtypeStruct((D_MODEL, n), BF16), tab_bf16, tab_bf16, tab_words, tab_words],
        scratch_shapes=[pltpu.VMEM((2, PEER_TOPK, tb), F32), pltpu.VMEM((2, PEER_TOPK, tb), F32)],
        compiler_params=_cparams(1),
    )(x, wq_t, bq_b, subkeys)


def _gelu_tanh(x):
    c0 = math.sqrt(2.0 / math.pi)
    inner = x * (jnp.asarray(c0, x.dtype) + jnp.asarray(c0 * 0.044715, x.dtype) * (x * x))
    hx = jnp.asarray(0.5, x.dtype) * x
    return hx + hx * jnp.tanh(inner)


def _peer_sweep_kernel(x_ref, xt_ref, rk_ref, e2_ref, e1_ref, ns_ref, u_ref, vt_ref, vec_ref,
                       o_ref, acc_ref):
    e = pl.program_id(1)

    @pl.when(e == 0)
    def _():
        acc_ref[...] = jnp.zeros_like(acc_ref)

    n_sub = PEER_EXP_TILE // PEER_NKEYS
    tb = xt_ref.shape[1]
    zero = jnp.zeros((), BF16)
    first = pl.multiple_of(e * n_sub, n_sub)

    def row_tile(ref, h, j, lanes):
        row = ref[h, pl.ds(first, n_sub), lanes][j:j + 1, :]
        tile = pltpu.bitcast(jnp.broadcast_to(row, (BF16_ROWS // 2, row.shape[1])), BF16)
        return jnp.concatenate([tile] * (PEER_NKEYS // BF16_ROWS), axis=0)

    sub = tb // PEER_TOK_SPLIT
    for part in range(PEER_TOK_SPLIT):
        lanes = slice(part * sub, (part + 1) * sub)
        xt = xt_ref[:, lanes]
        acts = []
        for j in range(n_sub):
            rows = slice(j * PEER_NKEYS, (j + 1) * PEER_NKEYS)
            gate = None
            for h in range(PEER_HEADS):
                keep = rk_ref[h, :, lanes] < row_tile(ns_ref, h, j, lanes)
                term = jnp.where(keep, e2_ref[h, :, lanes], zero) * row_tile(e1_ref, h, j, lanes)
                gate = term if gate is None else gate + term
            ht = _dot(u_ref[rows, :], xt)
            acts.append(gate * _gelu_tanh(ht.astype(BF16)))
        acc_ref[:, lanes] += _dot(vt_ref[...], jnp.concatenate(acts, axis=0))

    @pl.when(e == pl.num_programs(1) - 1)
    def _():
        c = acc_ref[...].T
        o_ref[...] = _layer_norm(DN_ALPHA * x_ref[...] + c, vec_ref[0:1, :], vec_ref[1:2, :])


def _peer_sweep(x, xt, rk, e2, e1, nsel, u_bf, vt_bf, vec):
    n = x.shape[0]
    tb, eb = PEER_TOK_TILE, PEER_EXP_TILE
    tab = pl.BlockSpec((PEER_HEADS, PEER_NKEYS, tb), lambda i, e: (0, 0, i))
    return pl.pallas_call(
        _peer_sweep_kernel,
        grid=(n // tb, PEER_NEXPERTS // eb),
        in_specs=[pl.BlockSpec((tb, D_MODEL), lambda i, e: (i, 0)),
                  pl.BlockSpec((D_MODEL, tb), lambda i, e: (0, i)),
                  tab, tab, tab, tab,
                  pl.BlockSpec((eb, D_MODEL), lambda i, e: (e, 0)),
                  pl.BlockSpec((D_MODEL, eb), lambda i, e: (0, e)),
                  pl.BlockSpec(vec.shape, lambda i, e: (0, 0))],
        out_specs=pl.BlockSpec((tb, D_MODEL), lambda i, e: (i, 0)),
        out_shape=jax.ShapeDtypeStruct((n, D_MODEL), F32),
        scratch_shapes=[pltpu.VMEM((D_MODEL, tb), F32)],
        compiler_params=_cparams(2),
    )(x, xt, rk, e2, e1, nsel, u_bf, vt_bf, vec)


def _ple_kernel(x_ref, p_ref, wp_ref, wg_ref, bg_ref, *rest, n_extra):
    we_refs, o_ref, e_refs = rest[:n_extra], rest[n_extra], rest[n_extra + 1:]
    x = x_ref[...]
    gate = jax.nn.sigmoid(_dot(x.astype(BF16), wg_ref[...]) + bg_ref[...])
    x3 = x + _dot(p_ref[...].astype(BF16), wp_ref[...]) * gate
    o_ref[...] = x3
    x3b = x3.astype(BF16)
    for we_ref, e_ref in zip(we_refs, e_refs):
        e_ref[...] = _dot(x3b, we_ref[...])


def _ple(x, p, w_p, w_g, b_g, w_extra=()):
    n = x.shape[0]
    tm = ROW_TILE
    row = pl.BlockSpec((tm, D_MODEL), lambda i: (i, 0))
    n_extra = len(w_extra)
    return pl.pallas_call(
        functools.partial(_ple_kernel, n_extra=n_extra),
        grid=(n // tm,),
        in_specs=[row, pl.BlockSpec((tm, p.shape[1]), lambda i: (i, 0)),
                  _full(w_p.shape), _full(w_g.shape), _full(b_g.shape)]
                 + [_full(w.shape) for w in w_extra],
        out_specs=[row] * (1 + n_extra),
        out_shape=[jax.ShapeDtypeStruct((n, D_MODEL), F32)] * (1 + n_extra),
        compiler_params=_cparams(1),
    )(x, p, w_p, w_g, b_g, *w_extra)


def _subln(of, g, lam_init):
    of = of * lax.rsqrt(jnp.mean(of * of, axis=-1, keepdims=True) + RMS_EPS) * g
    return of * (1.0 - lam_init)


def _attn_prompt_kernel(scal_ref, q_ref, k_ref, v_ref, g_ref, o_ref, kb_ref, vt_ref, bias_ref,
                        m_ref, l_ref, acc_ref, *, lam_init):
    h = pl.program_id(1)
    qi = pl.program_id(2)
    tq = ATT_TILE
    slope = scal_ref[h]
    lam = scal_ref[DA_HEADS]
    d0 = (lax.broadcasted_iota(jnp.int32, (tq, 2 * tq), 1) % tq
          - lax.broadcasted_iota(jnp.int32, (tq, 2 * tq), 0))

    @pl.when(qi == 0)
    def _():
        kb_ref[...] = k_ref[...].astype(BF16)
        vt_ref[...] = v_ref[...].T.astype(BF16)
        bias_ref[...] = slope * d0.astype(F32)

    qt = (q_ref[...] * (DA_HEAD ** -0.5)).T
    sub = lax.broadcasted_iota(jnp.int32, qt.shape, 0)
    qs = jnp.concatenate([jnp.where(sub < DA_HEAD, qt, 0.0), jnp.where(sub >= DA_HEAD, qt, 0.0)],
                         axis=1).astype(BF16)
    m_ref[...] = jnp.full_like(m_ref, -jnp.inf)
    l_ref[...] = jnp.zeros_like(l_ref)
    acc_ref[...] = jnp.zeros_like(acc_ref)

    def block(ki, bias, shift):
        off = pl.multiple_of(ki * tq, tq)
        s = _dot(kb_ref[pl.ds(off, tq), :], qs) - bias
        m_prev = m_ref[...]
        m_new = jnp.maximum(m_prev, jnp.max(s, axis=0, keepdims=True) - shift)
        alpha = jnp.exp(m_prev - m_new)
        p = jnp.exp(s - (m_new + shift))
        l_ref[...] = alpha * l_ref[...] + jnp.sum(p, axis=0, keepdims=True)
        acc_ref[...] = alpha * acc_ref[...] + _dot(vt_ref[:, pl.ds(off, tq)], p.astype(BF16))
        m_ref[...] = m_new

    def past(ki, carry):
        block(ki, bias_ref[...], slope * ((qi - ki) * tq).astype(F32))
        return carry

    lax.fori_loop(0, qi, past, 0)
    block(qi, jnp.where(d0 >= 0, bias_ref[...], -NEG_INF), 0.0)
    o = acc_ref[...] / l_ref[...]
    of = (o[:, :tq] - lam * o[:, tq:]).T
    o_ref[...] = _subln(of, g_ref[...], lam_init)


def _attn_prompt(scal, q, k, v, g, batch, seq, lam_init):
    tq = ATT_TILE
    nq = seq // tq
    grid_spec = pltpu.PrefetchScalarGridSpec(
        num_scalar_prefetch=1,
        grid=(batch, DA_HEADS, nq),
        in_specs=[pl.BlockSpec((tq, DA_VDIM), lambda b, h, i, s: (b * nq + i, h)),
                  pl.BlockSpec((seq, DA_VDIM), lambda b, h, i, s: (b, h)),
                  pl.BlockSpec((seq, DA_VDIM), lambda b, h, i, s: (b, h)),
                  pl.BlockSpec((1, DA_VDIM), lambda b, h, i, s: (0, 0))],
        out_specs=pl.BlockSpec((tq, DA_VDIM), lambda b, h, i, s: (b * nq + i, h)),
        scratch_shapes=[pltpu.VMEM((seq, DA_VDIM), BF16), pltpu.VMEM((DA_VDIM, seq), BF16),
                        pltpu.VMEM((tq, 2 * tq), F32),
                        pltpu.VMEM((1, 2 * tq), F32), pltpu.VMEM((1, 2 * tq), F32),
                        pltpu.VMEM((DA_VDIM, 2 * tq), F32)])
    return pl.pallas_call(
        functools.partial(_attn_prompt_kernel, lam_init=lam_init),
        grid_spec=grid_spec,
        out_shape=jax.ShapeDtypeStruct((batch * seq, D_MODEL), F32),
        compiler_params=_cparams(3),
    )(scal, q, k, v, g)


NEW_TOKENS_PAD = LANES // DA_HEADS


def _attn_sample_kernel(pt_ref, scal_ref, q_ref, *rest, dec_seq, page, lam_init):
    k_refs, v_refs = rest[:SAMPLE_PAGES], rest[SAMPLE_PAGES:2 * SAMPLE_PAGES]
    (kn_ref, vn_ref, bias_ref, biasn_ref, g_ref, o_ref, qs_ref, m_ref, l_ref, acc_ref
     ) = rest[2 * SAMPLE_PAGES:]
    step = pl.program_id(1)
    rows = 2 * dec_seq * DA_HEADS
    row_h = lax.broadcasted_iota(jnp.int32, (rows, 1), 0) % DA_HEADS
    slope = jnp.zeros((rows, 1), F32)
    for h in range(DA_HEADS):
        slope = jnp.where(row_h == h, scal_ref[h], slope)

    @pl.when(step == 0)
    def _():
        q = q_ref[0] * (DA_HEAD ** -0.5)
        lane = lax.broadcasted_iota(jnp.int32, q.shape, 1)
        qs_ref[...] = jnp.concatenate(
            [jnp.where(lane < DA_HEAD, q, 0.0), jnp.where(lane >= DA_HEAD, q, 0.0)],
            axis=0).astype(BF16)
        m_ref[...] = jnp.full_like(m_ref, -jnp.inf)
        l_ref[...] = jnp.zeros_like(l_ref)
        acc_ref[...] = jnp.zeros_like(acc_ref)

    def update(scores, values):
        m_prev = m_ref[...]
        m_new = functools.reduce(jnp.maximum,
                                 [jnp.max(s, axis=-1, keepdims=True) for s in scores], m_prev)
        alpha = jnp.exp(m_prev - m_new)
        l_new = alpha * l_ref[...]
        acc = alpha * acc_ref[...]
        for s, vb in zip(scores, values):
            pr = jnp.exp(s - m_new)
            l_new = l_new + jnp.sum(pr, axis=-1, keepdims=True)
            acc = acc + _dot(pr.astype(BF16), vb)
        m_ref[...] = m_new
        l_ref[...] = l_new
        acc_ref[...] = acc

    flat = lambda ref: ref[0].reshape(page * DA_HEADS, DA_VDIM).astype(BF16)
    scores = []
    for j in range(SAMPLE_PAGES):
        first_tok = ((step * SAMPLE_PAGES + j) * page).astype(F32)
        scores.append(_dot_nt(qs_ref[...], flat(k_refs[j])) - bias_ref[...] + slope * first_tok)
    update(scores, [flat(v_ref) for v_ref in v_refs])

    @pl.when(step == pl.num_programs(1) - 1)
    def _():
        update([_dot_nt(qs_ref[...], kn_ref[0].astype(BF16)) - biasn_ref[...]],
               [vn_ref[0].astype(BF16)])
        o = acc_ref[...] / l_ref[...]
        of = o[:rows // 2] - scal_ref[DA_HEADS] * o[rows // 2:]
        o_ref[0] = _subln(of, g_ref[...], lam_init)


def _attn_sample(page_table, scal, q, cache_k, cache_v, k_new, v_new, g, lam_init):
    b = q.shape[0]
    dec_seq = q.shape[1] // DA_HEADS
    n_pages = page_table.shape[1]
    page = cache_k.shape[1]
    past_len = n_pages * page
    rows = 2 * dec_seq * DA_HEADS
    assert n_pages % SAMPLE_PAGES == 0 and dec_seq <= NEW_TOKENS_PAD

    slopes = scal[:DA_HEADS]
    r = jnp.arange(rows)
    r_q, r_h = (r // DA_HEADS) % dec_seq, r % DA_HEADS

    def bias_tile(n_tok, first_q_pos, causal):
        col = jnp.arange(n_tok * DA_HEADS)
        dist = first_q_pos + r_q[:, None] - (col // DA_HEADS)[None, :]
        ok = (col % DA_HEADS)[None, :] == r_h[:, None]
        if causal:
            ok = ok & (dist >= 0)
        return jnp.where(ok, slopes[r_h][:, None] * dist.astype(F32), -NEG_INF)

    bias_past = bias_tile(page, past_len, False)
    bias_new = bias_tile(NEW_TOKENS_PAD, 0, True)

    def page_spec(j):
        return pl.BlockSpec((1, page, DA_HEADS, DA_VDIM),
                            lambda i, p, pt, s: (pt[i * n_pages + p * SAMPLE_PAGES + j], 0, 0, 0))

    per_seq = lambda shape: pl.BlockSpec((1,) + shape, lambda i, p, pt, s: (i, 0, 0))
    const = lambda shape: pl.BlockSpec(shape, lambda i, p, pt, s: (0, 0))
    grid_spec = pltpu.PrefetchScalarGridSpec(
        num_scalar_prefetch=2,
        grid=(b, n_pages // SAMPLE_PAGES),
        in_specs=[per_seq((dec_seq * DA_HEADS, DA_VDIM))]
                 + [page_spec(j) for j in range(SAMPLE_PAGES)] * 2
                 + [per_seq((LANES, DA_VDIM)), per_seq((LANES, DA_VDIM)),
                    const(bias_past.shape), const(bias_new.shape), const((1, DA_VDIM))],
        out_specs=per_seq((dec_seq * DA_HEADS, DA_VDIM)),
        scratch_shapes=[pltpu.VMEM((rows, DA_VDIM), BF16),
                        pltpu.VMEM((rows, 1), F32), pltpu.VMEM((rows, 1), F32),
                        pltpu.VMEM((rows, DA_VDIM), F32)])
    return pl.pallas_call(
        functools.partial(_attn_sample_kernel, dec_seq=dec_seq, page=page, lam_init=lam_init),
        grid_spec=grid_spec,
        out_shape=jax.ShapeDtypeStruct(q.shape, F32),
        compiler_params=_cparams(2),
    )(page_table.reshape(-1), scal, q, *([cache_k] * SAMPLE_PAGES), *([cache_v] * SAMPLE_PAGES),
      k_new, v_new, bias_past, bias_new, g)


def _lambda_init(layer):
    return 0.8 - 0.6 * math.exp(-0.3 * layer)


def _peer_weights(layer, peer_w_q, peer_b_q, peer_subkeys, peer_u, peer_v, ln2_g, ln2_b):
    wq_t = peer_w_q[layer].T.astype(BF16)
    bq_b = jnp.broadcast_to(peer_b_q[layer][:, None], (PEER_HEADS * PEER_QDIM, ROUTE_TILE))
    return (wq_t, bq_b, peer_subkeys[layer].astype(BF16), peer_u[layer].astype(BF16),
            peer_v[layer].astype(BF16).T, jnp.stack([ln2_g[layer], ln2_b[layer]]))


def _peer_layer(x, wq_t, bq_b, subkeys, u_bf, vt_bf, ln_vec):
    xt, rk, e2, e1, nsel = _peer_route(x, wq_t, bq_b, subkeys)
    return _peer_sweep(x, xt, rk, e2, e1, nsel, u_bf, vt_bf, ln_vec)


def kernel(x_prompt, x_sample, state_shift, state_wkv, cache_k, cache_v, page_table, p_prompt, p_sample, rw_mu, rw_w_r, rw_w_k, rw_w_v, rw_w0, rw_w1, rw_w2, rw_a0, rw_a1, rw_a2, rw_g1, rw_g2, rw_k_k, rw_k_a, rw_r_k, rw_gn_g, rw_gn_b, rw_w_o, da_w_k, da_w_v, da_w_q, da_lam, da_subln_g, da_w_o, ln1_g, ln1_b, ln2_g, ln2_b, peer_w_q, peer_b_q, peer_subkeys, peer_u, peer_v, ple_w_p, ple_w_g, ple_b_g):
    bp, tp, d = x_prompt.shape
    bs, ts, _ = x_sample.shape
    assert d == D_MODEL and bp * RW_HEADS == LANES and bs == LANES
    np_, ns = bp * tp, bs * ts
    hh, hd = RW_HEADS, RW_HEAD
    bf = lambda w: w.astype(BF16)

    head_of_lane = jnp.arange(d) // hd
    ind = (head_of_lane[:, None] == jnp.arange(LANES)[None, :]).astype(BF16)
    ind_t = ind.T
    pre_consts = (rw_mu[0], bf(rw_w_r[0]), bf(rw_w_k[0]), bf(rw_w_v[0]), bf(rw_w1[0]), bf(rw_w2[0]),
                  bf(rw_a1[0]), bf(rw_a2[0]), bf(rw_g1[0]), bf(rw_g2[0]),
                  jnp.stack([rw_w0[0], rw_a0[0], rw_k_k[0], rw_k_a[0]]), ind, ind_t)
    post_consts = (jnp.stack([rw_gn_g[0], rw_gn_b[0], rw_r_k[0].reshape(d), ln1_g[0], ln1_b[0]]),
                   ind, ind_t, bf(rw_w_o[0]))
    peer_consts = [_peer_weights(layer, peer_w_q, peer_b_q, peer_subkeys, peer_u, peer_v, ln2_g, ln2_b)
                   for layer in range(DEPTH)]
    ple_consts = [(bf(ple_w_p[layer]), bf(ple_w_g[layer]), ple_b_g[layer][None])
                  for layer in range(DEPTH)]
    kvq_w = (bf(da_w_k), bf(da_w_v), bf(da_w_q[0]))
    attn_out = (jnp.stack([ln1_g[1], ln1_b[1]]), bf(da_w_o[0]))
    lam_init = _lambda_init(1)
    lv = da_lam[0].astype(F32)
    lam = jnp.exp(jnp.sum(lv[0] * lv[1])) - jnp.exp(jnp.sum(lv[2] * lv[3])) + lam_init
    slopes = 2.0 ** (-8.0 * jnp.arange(1, DA_HEADS + 1, dtype=F32) / DA_HEADS)
    scal = jnp.concatenate([slopes, lam[None]]).astype(F32)
    subln_g = da_subln_g[0][None]

    def trunk(mixer, p, attend):
        x, state = mixer()
        x = _peer_layer(x, *peer_consts[0])
        x, k_sh, v_sh, q = _ple(x, p[0], *ple_consts[0], kvq_w)
        x = _proj_ln(x, attend(q, k_sh, v_sh), *attn_out)
        x = _peer_layer(x, *peer_consts[1])
        (x,) = _ple(x, p[1], *ple_consts[1])
        return x, state, k_sh, v_sh

    def prompt_mixer():
        r, w, k, v, a, b, g = _rwkv_pre_tm(x_prompt, jnp.zeros((bp, d), F32), *pre_consts)
        to_chains = lambda t: t.reshape(tp, LANES, hd).transpose(0, 2, 1)[None]
        y, state = _wkv_scan(*(to_chains(t) for t in (r, w, k, v, a, b)),
                             jnp.zeros((1, hd, hd, LANES), F32))
        y = y[0].transpose(0, 2, 1).reshape(np_, d)
        return _rwkv_post_tm(x_prompt, y, r, k, v, g, *post_consts).reshape(np_, d), state

    y_prompt, st_p, k_p, v_p = trunk(
        prompt_mixer, p_prompt.reshape(DEPTH, np_, -1),
        lambda q, k_sh, v_sh: _attn_prompt(scal, q, k_sh, v_sh, subln_g, bp, tp, lam_init))

    def sample_mixer():
        x = x_sample.reshape(ns, d)
        x_prev = jnp.concatenate([state_shift[0][:, None, :], x_sample[:, :-1]], axis=1).reshape(ns, d)
        r, w, k, v, a, b, g = _rwkv_pre(x, x_prev, *pre_consts)
        to_chains = lambda t: t.reshape(bs, ts, hh, hd).transpose(2, 1, 3, 0)
        y, state = _wkv_scan(*(to_chains(t) for t in (r, w, k, v, a, b)),
                             state_wkv[0].astype(F32).transpose(1, 2, 3, 0))
        y = y.transpose(3, 1, 0, 2).reshape(ns, d)
        return _rwkv_post(x, y, r, k, v, g, *post_consts), state

    new_rows = lambda t: jnp.pad(t.reshape(bs, ts, d), ((0, 0), (0, NEW_TOKENS_PAD - ts), (0, 0))
                                 ).reshape(bs, LANES, DA_VDIM)
    y_sample, st_s, k_s, v_s = trunk(
        sample_mixer, p_sample.reshape(DEPTH, ns, -1),
        lambda q, k_sh, v_sh: _attn_sample(
            page_table, scal, q.reshape(bs, ts * DA_HEADS, DA_VDIM), cache_k, cache_v,
            new_rows(k_sh), new_rows(v_sh), subln_g, lam_init).reshape(ns, d))

    wkv_p = st_p.reshape(hd, hd, bp, hh).transpose(2, 3, 0, 1)[None].astype(state_wkv.dtype)
    wkv_s = st_s.transpose(3, 0, 1, 2)[None].astype(state_wkv.dtype)
    kv_shape_p = (bp, tp, DA_HEADS, DA_VDIM)
    kv_shape_s = (bs, ts, DA_HEADS, DA_VDIM)
    return (y_prompt.reshape(bp, tp, d), y_sample.reshape(bs, ts, d),
            x_prompt[:, -1][None], wkv_p, k_p.reshape(kv_shape_p), v_p.reshape(kv_shape_p),
            x_sample[:, -1][None], wkv_s, k_s.reshape(kv_shape_s), v_s.reshape(kv_shape_s))
```

```python
import functools
import math

import jax
import jax.numpy as jnp
from jax import lax
from jax.experimental import pallas as pl
from jax.experimental.pallas import tpu as pltpu

F32 = jnp.float32
BF16 = jnp.bfloat16

D_MODEL = 1024
DEPTH = 2
RW_HEAD = 64
RW_HEADS = D_MODEL // RW_HEAD
RW_GN_EPS = 64e-5
DA_HEAD = 64
DA_HEADS = D_MODEL // (2 * DA_HEAD)
DA_VDIM = 2 * DA_HEAD
RMS_EPS = 1e-5
NEG_INF = -1e30
PEER_HEADS = 8
PEER_NKEYS = 128
PEER_NEXPERTS = PEER_NKEYS * PEER_NKEYS
PEER_QDIM = 256
PEER_HALF = PEER_QDIM // 2
PEER_TOPK = 16
DN_ALPHA = (2.0 * DEPTH) ** 0.25
LN_EPS = 1e-5

LANES = 128
BF16_ROWS = 16
VMEM_LIMIT = 56 * 1024 * 1024

ROW_TILE = 256
ROUTE_TILE = 256
PEER_TOK_TILE = 512
PEER_TOK_SPLIT = 1
PEER_EXP_TILE = 2048
SCAN_TCHUNK = 32
SCAN_VROWS = 4
SCAN_UNROLL = 8
ATT_TILE = 512
SAMPLE_PAGES = 16


def _cparams(n_axes):
    return pltpu.CompilerParams(dimension_semantics=("arbitrary",) * n_axes,
                                vmem_limit_bytes=VMEM_LIMIT)


def _dot(a, b):
    return jnp.dot(a, b, preferred_element_type=F32)


def _dot_nt(a, b):
    return lax.dot_general(a, b, (((1,), (1,)), ((), ())), preferred_element_type=F32)


def _layer_norm(x, g, b):
    mu = jnp.mean(x, axis=-1, keepdims=True)
    xc = x - mu
    var = jnp.mean(xc * xc, axis=-1, keepdims=True)
    return xc * lax.rsqrt(var + LN_EPS) * g + b


def _split2(x):
    hi = x.astype(BF16)
    lo = (x - hi.astype(F32)).astype(BF16)
    return hi, lo


def _head_sum(x, ind, ind_t):
    hi, lo = _split2(x)
    s = _dot(hi, ind) + _dot(lo, ind)
    shi, slo = _split2(s)
    return _dot(shi, ind_t) + _dot(slo, ind_t)


def _full(shape):
    n = len(shape)
    return pl.BlockSpec(shape, lambda *_: (0,) * n)


def _rwkv_pre_kernel(x_ref, xp_ref, *refs):
    _rwkv_pre_math(x_ref[...], xp_ref[...], *refs)


def _batch_to_time_major(ref, scratch):
    nb, steps, dm = ref.shape
    tiles = range(dm // LANES)
    for b in range(nb):
        blk = ref[b]
        for c in tiles:
            scratch[c, b * steps:(b + 1) * steps, :] = blk[:, c * LANES:(c + 1) * LANES]
    return jnp.concatenate(
        [jnp.concatenate([scratch[c, pl.ds(t, nb, stride=steps), :] for c in tiles], axis=1)
         for t in range(steps)], axis=0)


def _time_to_batch_major(val, ref, scratch):
    nb, steps, dm = ref.shape
    tiles = range(dm // LANES)
    for t in range(steps):
        for c in tiles:
            scratch[c, pl.ds(t, nb, stride=steps), :] = val[t * nb:(t + 1) * nb, c * LANES:(c + 1) * LANES]
    for b in range(nb):
        ref[b] = jnp.concatenate([scratch[c, b * steps:(b + 1) * steps, :] for c in tiles], axis=1)


def _rwkv_pre_tm_kernel(x_ref, xl_ref, shift_ref, *refs):
    *refs, flat_ref = refs
    nb = x_ref.shape[0]
    x = _batch_to_time_major(x_ref, flat_ref)
    last = jnp.concatenate([xl_ref[b, xl_ref.shape[1] - 1:, :] for b in range(nb)], axis=0)
    first = jnp.where(pl.program_id(0) == 0, shift_ref[...], last)
    _rwkv_pre_math(x, jnp.concatenate([first, x[:-nb]], axis=0), *refs)


def _rwkv_pre_math(x, xp, mu_ref, wr_ref, wk_ref, wv_ref, w1_ref, w2_ref, a1_ref, a2_ref,
                   g1_ref, g2_ref, vec_ref, ind_ref, indt_ref,
                   r_ref, w_ref, k_ref, v_ref, a_ref, b_ref, g_ref):
    xx = xp - x
    mix = lambda i: (x + xx * mu_ref[i:i + 1, :]).astype(BF16)
    xr, xw, xk, xv, xa, xg = (mix(i) for i in range(6))
    w0, a0, k_k, k_a = (vec_ref[i:i + 1, :] for i in range(4))
    r = _dot(xr, wr_ref[...])
    k = _dot(xk, wk_ref[...])
    v = _dot(xv, wv_ref[...])
    wl = w0 + _dot(jnp.tanh(_dot(xw, w1_ref[...])).astype(BF16), w2_ref[...])
    z = -wl
    softplus = jnp.maximum(z, 0.0) + jnp.log(1.0 + jnp.exp(-jnp.abs(z)))
    decay = jnp.exp(-jnp.exp(-softplus - 0.5))
    a = jax.nn.sigmoid(a0 + _dot(_dot(xa, a1_ref[...]).astype(BF16), a2_ref[...]))
    g = _dot(jax.nn.sigmoid(_dot(xg, g1_ref[...])).astype(BF16), g2_ref[...])
    kk = k * k_k
    n2 = _head_sum(kk * kk, ind_ref[...], indt_ref[...])
    kk = kk / jnp.maximum(jnp.sqrt(n2), 1e-12)
    r_ref[...] = r
    w_ref[...] = decay
    k_ref[...] = k * (1.0 + (a - 1.0) * k_a)
    v_ref[...] = v
    a_ref[...] = -kk
    b_ref[...] = kk * a
    g_ref[...] = g


def _rwkv_pre(x, xp, mu, w_r, w_k, w_v, w1, w2, a1, a2, g1, g2, vec, ind, ind_t):
    n = x.shape[0]
    tm = ROW_TILE
    row = pl.BlockSpec((tm, D_MODEL), lambda i: (i, 0))
    consts = (mu, w_r, w_k, w_v, w1, w2, a1, a2, g1, g2, vec, ind, ind_t)
    return pl.pallas_call(
        _rwkv_pre_kernel,
        grid=(n // tm,),
        in_specs=[row, row] + [_full(c.shape) for c in consts],
        out_specs=[row] * 7,
        out_shape=[jax.ShapeDtypeStruct((n, D_MODEL), F32)] * 7,
        compiler_params=_cparams(1),
    )(x, xp, *consts)


TM_LAST = 8


def _rwkv_pre_tm(x, shift, *consts):
    nb, t, _ = x.shape
    steps = ROW_TILE // nb
    row = pl.BlockSpec((ROW_TILE, D_MODEL), lambda i: (i, 0))
    return pl.pallas_call(
        _rwkv_pre_tm_kernel,
        grid=(t // steps,),
        in_specs=[pl.BlockSpec((nb, steps, D_MODEL), lambda i: (0, i, 0)),
                  pl.BlockSpec((nb, TM_LAST, D_MODEL),
                               lambda i: (0, jnp.maximum(i * (steps // TM_LAST) - 1, 0), 0)),
                  _full(shift.shape)] + [_full(c.shape) for c in consts],
        out_specs=[row] * 7,
        out_shape=[jax.ShapeDtypeStruct((t * nb, D_MODEL), F32)] * 7,
        scratch_shapes=[pltpu.VMEM((D_MODEL // LANES, ROW_TILE, LANES), F32)],
        compiler_params=_cparams(1),
    )(x, x, shift, *consts)


def _scan_chunk(r_ref, w_ref, k_ref, v_ref, a_ref, b_ref, y_ref, st_ref, wr_ref, dot_ref, tchunk):
    def prep(t, carry):
        r = r_ref[t]
        wr_ref[t] = w_ref[t] * r
        dot_ref[t, 0:1, :] = jnp.sum(b_ref[t] * r, axis=0, keepdims=True)
        dot_ref[t, 1:2, :] = jnp.sum(k_ref[t] * r, axis=0, keepdims=True)
        return carry

    lax.fori_loop(0, tchunk, prep, 0)

    def vloop(vc, carry):
        v0 = vc * SCAN_VROWS
        state = tuple(st_ref[v0 + j] for j in range(SCAN_VROWS))

        def step(t, st):
            a = a_ref[t]
            wr = wr_ref[t]
            w = w_ref[t]
            b = b_ref[t]
            k = k_ref[t]
            br = dot_ref[t, 0:1, :]
            kr = dot_ref[t, 1:2, :]
            out = []
            for j in range(SCAN_VROWS):
                s = st[j]
                val = v_ref[t, pl.ds(v0 + j, 1), :]
                sa = jnp.sum(s * a, axis=0, keepdims=True)
                y0 = jnp.sum(s * wr, axis=0, keepdims=True)
                y_ref[t, pl.ds(v0 + j, 1), :] = y0 + sa * br + val * kr
                out.append(s * w + sa * b + val * k)
            return tuple(out)

        state = lax.fori_loop(0, tchunk, step, state, unroll=SCAN_UNROLL)
        for j in range(SCAN_VROWS):
            st_ref[v0 + j] = state[j]
        return carry

    lax.fori_loop(0, RW_HEAD // SCAN_VROWS, vloop, 0)


def _scan_kernel(r_ref, w_ref, k_ref, v_ref, a_ref, b_ref, s0_ref, y_ref, st_ref, wr_ref, dot_ref,
                 *, tchunk):
    @pl.when(pl.program_id(1) == 0)
    def _():
        st_ref[...] = s0_ref[...]

    _scan_chunk(*(ref.at[0] for ref in (r_ref, w_ref, k_ref, v_ref, a_ref, b_ref, y_ref, st_ref)),
                wr_ref, dot_ref, tchunk)


def _slabs_to_tiles(slab0, slab1):
    stack = lambda slab: [slab[:, g * LANES:(g + 1) * LANES] for g in range(slab.shape[1] // LANES)]
    t = jnp.concatenate(stack(slab0) + stack(slab1), axis=0).T
    even, odd = t[:RW_HEAD], t[RW_HEAD:]
    low = lax.broadcasted_iota(jnp.int32, even.shape, 1) < RW_HEAD
    return (jnp.where(low, even, pltpu.roll(odd, RW_HEAD, axis=1)),
            jnp.where(low, pltpu.roll(even, RW_HEAD, axis=1), odd))


def _tiles_to_slabs(tile0, tile1, nb):
    low = lax.broadcasted_iota(jnp.int32, tile0.shape, 1) < RW_HEAD
    even = jnp.where(low, tile0, pltpu.roll(tile1, RW_HEAD, axis=1))
    odd = jnp.where(low, pltpu.roll(tile0, RW_HEAD, axis=1), tile1)
    t = jnp.concatenate([even, odd], axis=0).T
    slab = lambda rows: jnp.concatenate([rows[g * nb:(g + 1) * nb] for g in range(RW_HEAD // nb)], axis=1)
    return slab(t[:RW_HEAD]), slab(t[RW_HEAD:])


def _scan_tm_kernel(r_ref, w_ref, k_ref, v_ref, a_ref, b_ref, y_ref, st_ref,
                    ops_ref, ych_ref, wr_ref, dot_ref, *, tchunk, nb):
    @pl.when(pl.program_id(0) == 0)
    def _():
        st_ref[...] = jnp.zeros_like(st_ref)

    rows = lambda t: pl.ds(pl.multiple_of(t * nb, nb), nb)

    def load(p, carry):
        t = 2 * p
        for i, ref in enumerate((r_ref, w_ref, k_ref, v_ref, a_ref, b_ref)):
            ops_ref[i, t], ops_ref[i, t + 1] = _slabs_to_tiles(ref[rows(t), :], ref[rows(t + 1), :])
        return carry

    lax.fori_loop(0, tchunk // 2, load, 0, unroll=2)
    _scan_chunk(*(ops_ref.at[i] for i in range(6)), ych_ref, st_ref, wr_ref, dot_ref, tchunk)

    def store(p, carry):
        t = 2 * p
        y_ref[rows(t), :], y_ref[rows(t + 1), :] = _tiles_to_slabs(ych_ref[t], ych_ref[t + 1], nb)
        return carry

    lax.fori_loop(0, tchunk // 2, store, 0, unroll=2)


def _wkv_scan_tm(r, w, k, v, a, b, nb):
    assert nb * RW_HEADS == LANES and nb == SUBLANES
    t = r.shape[0] // nb
    tchunk = SCAN_TCHUNK
    seq = pl.BlockSpec((tchunk * nb, D_MODEL), lambda ti: (ti, 0))
    st = pl.BlockSpec((RW_HEAD, RW_HEAD, LANES), lambda ti: (0, 0, 0))
    tile = (tchunk, RW_HEAD, LANES)
    return pl.pallas_call(
        functools.partial(_scan_tm_kernel, tchunk=tchunk, nb=nb),
        grid=(t // tchunk,),
        in_specs=[seq] * 6,
        out_specs=[seq, st],
        out_shape=[jax.ShapeDtypeStruct(r.shape, F32),
                   jax.ShapeDtypeStruct((RW_HEAD, RW_HEAD, LANES), F32)],
        scratch_shapes=[pltpu.VMEM((6,) + tile, F32), pltpu.VMEM(tile, F32), pltpu.VMEM(tile, F32),
                        pltpu.VMEM((tchunk, 8, LANES), F32)],
        compiler_params=_cparams(1),
    )(r, w, k, v, a, b)


def _wkv_scan(r, w, k, v, a, b, s0):
    g, t = r.shape[0], r.shape[1]
    tchunk = min(SCAN_TCHUNK, t)
    seq = pl.BlockSpec((1, tchunk, RW_HEAD, LANES), lambda gi, ti: (gi, ti, 0, 0))
    st = pl.BlockSpec((1, RW_HEAD, RW_HEAD, LANES), lambda gi, ti: (gi, 0, 0, 0))
    return pl.pallas_call(
        functools.partial(_scan_kernel, tchunk=tchunk),
        grid=(g, t // tchunk),
        in_specs=[seq] * 6 + [st],
        out_specs=[seq, st],
        out_shape=[jax.ShapeDtypeStruct(r.shape, F32), jax.ShapeDtypeStruct(s0.shape, F32)],
        scratch_shapes=[pltpu.VMEM((tchunk, RW_HEAD, LANES), F32), pltpu.VMEM((tchunk, 8, LANES), F32)],
        compiler_params=_cparams(2),
    )(r, w, k, v, a, b, s0)


def _rwkv_post_math(x, y_ref, r_ref, k_ref, v_ref, g_ref, vec_ref, ind_ref, indt_ref, wo_ref):
    ind, ind_t = ind_ref[...], indt_ref[...]
    gn_g, gn_b, r_k, ln_g, ln_b = (vec_ref[i:i + 1, :] for i in range(5))
    y = y_ref[...]
    inv = 1.0 / RW_HEAD
    yc = y - _head_sum(y, ind, ind_t) * inv
    var = _head_sum(yc * yc, ind, ind_t) * inv
    yn = yc * lax.rsqrt(var + RW_GN_EPS) * gn_g + gn_b
    v = v_ref[...]
    bonus = _head_sum(r_ref[...] * k_ref[...] * r_k, ind, ind_t) * v
    h = _dot(((yn + bonus) * g_ref[...]).astype(BF16), wo_ref[...])
    return _layer_norm(DN_ALPHA * x + h, ln_g, ln_b)


def _rwkv_post_kernel(x_ref, *refs):
    *refs, o_ref = refs
    o_ref[...] = _rwkv_post_math(x_ref[...], *refs)


def _rwkv_post_tm_kernel(x_ref, *refs):
    *refs, o_ref, flat_ref = refs
    nb, steps, _ = x_ref.shape
    out = _rwkv_post_math(_batch_to_time_major(x_ref, flat_ref), *refs)
    _time_to_batch_major(out, o_ref, flat_ref)


def _rwkv_post_tm(x, y, r, k, v, g, *consts):
    nb, t, _ = x.shape
    steps = ROW_TILE // nb
    row = pl.BlockSpec((ROW_TILE, D_MODEL), lambda i: (i, 0))
    blk = pl.BlockSpec((nb, steps, D_MODEL), lambda i: (0, i, 0))
    return pl.pallas_call(
        _rwkv_post_tm_kernel,
        grid=(t // steps,),
        in_specs=[blk] + [row] * 5 + [_full(c.shape) for c in consts],
        out_specs=blk,
        out_shape=jax.ShapeDtypeStruct(x.shape, F32),
        scratch_shapes=[pltpu.VMEM((D_MODEL // LANES, ROW_TILE, LANES), F32)],
        compiler_params=_cparams(1),
    )(x, y, r, k, v, g, *consts)


def _rwkv_post(x, y, r, k, v, g, vec, ind, ind_t, w_o):
    n = x.shape[0]
    tm = ROW_TILE
    row = pl.BlockSpec((tm, D_MODEL), lambda i: (i, 0))
    consts = (vec, ind, ind_t, w_o)
    return pl.pallas_call(
        _rwkv_post_kernel,
        grid=(n // tm,),
        in_specs=[row] * 6 + [_full(c.shape) for c in consts],
        out_specs=row,
        out_shape=jax.ShapeDtypeStruct((n, D_MODEL), F32),
        compiler_params=_cparams(1),
    )(x, y, r, k, v, g, *consts)


def _proj_ln_kernel(x_ref, h_ref, vec_ref, wo_ref, o_ref):
    h = _dot(h_ref[...].astype(BF16), wo_ref[...])
    o_ref[...] = _layer_norm(DN_ALPHA * x_ref[...] + h, vec_ref[0:1, :], vec_ref[1:2, :])


def _proj_ln(x, h, vec, w_o):
    n = x.shape[0]
    tm = ROW_TILE
    row = pl.BlockSpec((tm, D_MODEL), lambda i: (i, 0))
    return pl.pallas_call(
        _proj_ln_kernel,
        grid=(n // tm,),
        in_specs=[row, row, _full(vec.shape), _full(w_o.shape)],
        out_specs=row,
        out_shape=jax.ShapeDtypeStruct((n, D_MODEL), F32),
        compiler_params=_cparams(1),
    )(x, h, vec, w_o)


SUBLANES = 8


def _sort_network(n):
    def merge(lo, hi, r):
        step = r * 2
        if step < hi - lo:
            yield from merge(lo, hi, step)
            yield from merge(lo + r, hi, step)
            yield from [(i, i + r) for i in range(lo + r, hi - r, step)]
        else:
            yield (lo, lo + r)

    def sort(lo, hi):
        if hi - lo >= 1:
            mid = lo + (hi - lo) // 2
            yield from sort(lo, mid)
            yield from sort(mid + 1, hi)
            yield from merge(lo, hi, 1)

    return list(sort(0, n - 1))


def _bf16_pair_words(x):
    bits = lax.bitcast_convert_type(x.astype(BF16).astype(F32), jnp.int32)
    return bits | lax.shift_right_logical(bits, jnp.full_like(bits, 16))


def _top_list(s, top_ref, cnt_ref, c):
    stack = [s[g * SUBLANES:(g + 1) * SUBLANES, :] for g in range(s.shape[0] // SUBLANES)]
    for i, j in _sort_network(len(stack)):
        stack[i], stack[j] = jnp.maximum(stack[i], stack[j]), jnp.minimum(stack[i], stack[j])
    for j in range(PEER_TOPK):
        m = jnp.max(stack[0], axis=0, keepdims=True)
        hit = stack[0] >= m
        top_ref[c, j:j + 1, :] = m
        cnt_ref[c, j:j + 1, :] = jnp.sum(jnp.where(hit, 1.0, 0.0), axis=0, keepdims=True)
        for r in range(PEER_TOPK - 1 - j):
            stack[r] = jnp.where(hit, stack[r + 1], stack[r])


def _peer_route_kernel(x_ref, wq_ref, bq_ref, sk_ref, xt_ref, rk_ref, e2_ref, e1_ref, ns_ref,
                       top_ref, cnt_ref):
    xt = x_ref[...].T.astype(BF16)
    xt_ref[...] = xt
    q = _dot(wq_ref[...], xt) + bq_ref[...]
    k = PEER_TOPK
    assert PEER_NKEYS // SUBLANES >= k and k == 2 * SUBLANES
    for h in range(PEER_HEADS):
        sc = []
        for c in range(2):
            lo = (2 * h + c) * PEER_HALF
            s = _dot(sk_ref[c], q[lo:lo + PEER_HALF, :].astype(BF16))
            sc.append(s)
            _top_list(s, top_ref, cnt_ref, c)
        ta, tb = top_ref[0], top_ref[1]
        ca, cb = cnt_ref[0], cnt_ref[1]
        lo_a, hi_a = ta[:SUBLANES], ta[SUBLANES:]
        cand = [lo_a + tb[j:j + 1, :] for j in range(k)]
        mult = [ca[:SUBLANES] * cb[j:j + 1, :] for j in range(k)]
        cand_hi = hi_a + tb[0:1, :]
        mult_hi = ca[SUBLANES:] * cb[0:1, :]
        best = thr = zsum = None
        remaining = jnp.full((1, xt.shape[1]), float(k), F32)
        for j in range(k):
            m = jnp.max(jnp.maximum(cand[0], cand_hi), axis=0, keepdims=True)
            hit, hit_hi = cand[0] >= m, cand_hi >= m
            n_hit = jnp.where(hit, mult[0], 0.0) + jnp.where(hit_hi, mult_hi, 0.0)
            take = jnp.minimum(jnp.sum(n_hit, axis=0, keepdims=True), remaining)
            if j == 0:
                best, thr, zsum = m, m, take
            else:
                zsum = zsum + take * jnp.exp(m - best)
                thr = jnp.where(take > 0.0, m, thr)
            remaining = remaining - take
            for r in range(k - 1 - j):
                cand[r] = jnp.where(hit, cand[r + 1], cand[r])
                mult[r] = jnp.where(hit, mult[r + 1], mult[r])
            cand_hi = jnp.where(hit_hi, -jnp.inf, cand_hi)
        s1, s2 = sc
        n_slot = jnp.zeros_like(ta)
        for j in range(k):
            n_slot = n_slot + jnp.where(ta + tb[j:j + 1, :] >= thr, 1.0, 0.0)
        nsel = jnp.zeros_like(s1)
        rank = jnp.full_like(s2, float(k))
        for j in reversed(range(k)):
            nsel = jnp.where(s1 >= ta[j:j + 1, :], n_slot[j:j + 1, :], nsel)
            rank = jnp.where(s2 >= tb[j:j + 1, :], float(j), rank)
        rk_ref[h] = rank.astype(BF16)
        ns_ref[h] = _bf16_pair_words(nsel)
        e1_ref[h] = _bf16_pair_words(jnp.exp(s1 - ta[0:1, :]))
        e2_ref[h] = (jnp.exp(s2 - tb[0:1, :]) / zsum).astype(BF16)


def _peer_route(x, wq_t, bq_b, subkeys):
    n = x.shape[0]
    tb = ROUTE_TILE
    tab = pl.BlockSpec((PEER_HEADS, PEER_NKEYS, tb), lambda i: (0, 0, i))
    tab_words = jax.ShapeDtypeStruct((PEER_HEADS, PEER_NKEYS, n), jnp.int32)
    tab_bf16 = jax.ShapeDtypeStruct((PEER_HEADS, PEER_NKEYS, n), BF16)
    return pl.pallas_call(
        _peer_route_kernel,
        grid=(n // tb,),
        in_specs=[pl.BlockSpec((tb, D_MODEL), lambda i: (i, 0)),
                  _full(wq_t.shape), _full(bq_b.shape), _full(subkeys.shape)],
        out_specs=[pl.BlockSpec((D_MODEL, tb), lambda i: (0, i)), tab, tab, tab, tab],
        out_shape=[jax.ShapeDtypeStruct((D_MODEL, n), BF16), tab_bf16, tab_bf16, tab_words, tab_words],
        scratch_shapes=[pltpu.VMEM((2, PEER_TOPK, tb), F32), pltpu.VMEM((2, PEER_TOPK, tb), F32)],
        compiler_params=_cparams(1),
    )(x, wq_t, bq_b, subkeys)


def _gelu_tanh(x):
    c0 = math.sqrt(2.0 / math.pi)
    inner = x * (jnp.asarray(c0, x.dtype) + jnp.asarray(c0 * 0.044715, x.dtype) * (x * x))
    hx = jnp.asarray(0.5, x.dtype) * x
    return hx + hx * jnp.tanh(inner)


def _peer_sweep_kernel(x_ref, xt_ref, rk_ref, e2_ref, e1_ref, ns_ref, u_ref, vt_ref, vec_ref,
                       o_ref, acc_ref):
    e = pl.program_id(1)

    @pl.when(e == 0)
    def _():
        acc_ref[...] = jnp.zeros_like(acc_ref)

    n_sub = PEER_EXP_TILE // PEER_NKEYS
    tb = xt_ref.shape[1]
    zero = jnp.zeros((), BF16)
    first = pl.multiple_of(e * n_sub, n_sub)

    def row_tile(ref, h, j, lanes):
        row = ref[h, pl.ds(first, n_sub), lanes][j:j + 1, :]
        tile = pltpu.bitcast(jnp.broadcast_to(row, (BF16_ROWS // 2, row.shape[1])), BF16)
        return jnp.concatenate([tile] * (PEER_NKEYS // BF16_ROWS), axis=0)

    sub = tb // PEER_TOK_SPLIT
    for part in range(PEER_TOK_SPLIT):
        lanes = slice(part * sub, (part + 1) * sub)
        xt = xt_ref[:, lanes]
        acts = []
        for j in range(n_sub):
            rows = slice(j * PEER_NKEYS, (j + 1) * PEER_NKEYS)
            gate = None
            for h in range(PEER_HEADS):
                keep = rk_ref[h, :, lanes] < row_tile(ns_ref, h, j, lanes)
                term = jnp.where(keep, e2_ref[h, :, lanes], zero) * row_tile(e1_ref, h, j, lanes)
                gate = term if gate is None else gate + term
            ht = _dot(u_ref[rows, :], xt)
            acts.append(gate * _gelu_tanh(ht.astype(BF16)))
        acc_ref[:, lanes] += _dot(vt_ref[...], jnp.concatenate(acts, axis=0))

    @pl.when(e == pl.num_programs(1) - 1)
    def _():
        c = acc_ref[...].T
        o_ref[...] = _layer_norm(DN_ALPHA * x_ref[...] + c, vec_ref[0:1, :], vec_ref[1:2, :])


def _peer_sweep(x, xt, rk, e2, e1, nsel, u_bf, vt_bf, vec):
    n = x.shape[0]
    tb, eb = PEER_TOK_TILE, PEER_EXP_TILE
    tab = pl.BlockSpec((PEER_HEADS, PEER_NKEYS, tb), lambda i, e: (0, 0, i))
    return pl.pallas_call(
        _peer_sweep_kernel,
        grid=(n // tb, PEER_NEXPERTS // eb),
        in_specs=[pl.BlockSpec((tb, D_MODEL), lambda i, e: (i, 0)),
                  pl.BlockSpec((D_MODEL, tb), lambda i, e: (0, i)),
                  tab, tab, tab, tab,
                  pl.BlockSpec((eb, D_MODEL), lambda i, e: (e, 0)),
                  pl.BlockSpec((D_MODEL, eb), lambda i, e: (0, e)),
                  pl.BlockSpec(vec.shape, lambda i, e: (0, 0))],
        out_specs=pl.BlockSpec((tb, D_MODEL), lambda i, e: (i, 0)),
        out_shape=jax.ShapeDtypeStruct((n, D_MODEL), F32),
        scratch_shapes=[pltpu.VMEM((D_MODEL, tb), F32)],
        compiler_params=_cparams(2),
    )(x, xt, rk, e2, e1, nsel, u_bf, vt_bf, vec)


def _ple_kernel(x_ref, p_ref, wp_ref, wg_ref, bg_ref, *rest, n_extra):
    we_refs, o_ref, e_refs = rest[:n_extra], rest[n_extra], rest[n_extra + 1:]
    x = x_ref[...]
    gate = jax.nn.sigmoid(_dot(x.astype(BF16), wg_ref[...]) + bg_ref[...])
    x3 = x + _dot(p_ref[...].astype(BF16), wp_ref[...]) * gate
    o_ref[...] = x3
    x3b = x3.astype(BF16)
    for we_ref, e_ref in zip(we_refs, e_refs):
        e_ref[...] = _dot(x3b, we_ref[...])


def _ple(x, p, w_p, w_g, b_g, w_extra=()):
    n = x.shape[0]
    tm = ROW_TILE
    row = pl.BlockSpec((tm, D_MODEL), lambda i: (i, 0))
    n_extra = len(w_extra)
    return pl.pallas_call(
        functools.partial(_ple_kernel, n_extra=n_extra),
        grid=(n // tm,),
        in_specs=[row, pl.BlockSpec((tm, p.shape[1]), lambda i: (i, 0)),
                  _full(w_p.shape), _full(w_g.shape), _full(b_g.shape)]
                 + [_full(w.shape) for w in w_extra],
        out_specs=[row] * (1 + n_extra),
        out_shape=[jax.ShapeDtypeStruct((n, D_MODEL), F32)] * (1 + n_extra),
        compiler_params=_cparams(1),
    )(x, p, w_p, w_g, b_g, *w_extra)


def _subln(of, g, lam_init):
    of = of * lax.rsqrt(jnp.mean(of * of, axis=-1, keepdims=True) + RMS_EPS) * g
    return of * (1.0 - lam_init)


def _attn_prompt_kernel(scal_ref, q_ref, k_ref, v_ref, g_ref, o_ref, kb_ref, vt_ref, bias_ref,
                        m_ref, l_ref, acc_ref, *, lam_init):
    h = pl.program_id(1)
    qi = pl.program_id(2)
    tq = ATT_TILE
    slope = scal_ref[h]
    lam = scal_ref[DA_HEADS]
    d0 = (lax.broadcasted_iota(jnp.int32, (tq, 2 * tq), 1) % tq
          - lax.broadcasted_iota(jnp.int32, (tq, 2 * tq), 0))

    @pl.when(qi == 0)
    def _():
        kb_ref[...] = k_ref[...].astype(BF16)
        vt_ref[...] = v_ref[...].T.astype(BF16)
        bias_ref[...] = slope * d0.astype(F32)

    qt = (q_ref[...] * (DA_HEAD ** -0.5)).T
    sub = lax.broadcasted_iota(jnp.int32, qt.shape, 0)
    qs = jnp.concatenate([jnp.where(sub < DA_HEAD, qt, 0.0), jnp.where(sub >= DA_HEAD, qt, 0.0)],
                         axis=1).astype(BF16)
    m_ref[...] = jnp.full_like(m_ref, -jnp.inf)
    l_ref[...] = jnp.zeros_like(l_ref)
    acc_ref[...] = jnp.zeros_like(acc_ref)

    def block(ki, bias, shift):
        off = pl.multiple_of(ki * tq, tq)
        s = _dot(kb_ref[pl.ds(off, tq), :], qs) - bias
        m_prev = m_ref[...]
        m_new = jnp.maximum(m_prev, jnp.max(s, axis=0, keepdims=True) - shift)
        alpha = jnp.exp(m_prev - m_new)
        p = jnp.exp(s - (m_new + shift))
        l_ref[...] = alpha * l_ref[...] + jnp.sum(p, axis=0, keepdims=True)
        acc_ref[...] = alpha * acc_ref[...] + _dot(vt_ref[:, pl.ds(off, tq)], p.astype(BF16))
        m_ref[...] = m_new

    def past(ki, carry):
        block(ki, bias_ref[...], slope * ((qi - ki) * tq).astype(F32))
        return carry

    lax.fori_loop(0, qi, past, 0)
    block(qi, jnp.where(d0 >= 0, bias_ref[...], -NEG_INF), 0.0)
    o = acc_ref[...] / l_ref[...]
    of = (o[:, :tq] - lam * o[:, tq:]).T
    o_ref[...] = _subln(of, g_ref[...], lam_init)


def _attn_prompt(scal, q, k, v, g, batch, seq, lam_init):
    tq = ATT_TILE
    nq = seq // tq
    grid_spec = pltpu.PrefetchScalarGridSpec(
        num_scalar_prefetch=1,
        grid=(batch, DA_HEADS, nq),
        in_specs=[pl.BlockSpec((tq, DA_VDIM), lambda b, h, i, s: (b * nq + i, h)),
                  pl.BlockSpec((seq, DA_VDIM), lambda b, h, i, s: (b, h)),
                  pl.BlockSpec((seq, DA_VDIM), lambda b, h, i, s: (b, h)),
                  pl.BlockSpec((1, DA_VDIM), lambda b, h, i, s: (0, 0))],
        out_specs=pl.BlockSpec((tq, DA_VDIM), lambda b, h, i, s: (b * nq + i, h)),
        scratch_shapes=[pltpu.VMEM((seq, DA_VDIM), BF16), pltpu.VMEM((DA_VDIM, seq), BF16),
                        pltpu.VMEM((tq, 2 * tq), F32),
                        pltpu.VMEM((1, 2 * tq), F32), pltpu.VMEM((1, 2 * tq), F32),
                        pltpu.VMEM((DA_VDIM, 2 * tq), F32)])
    return pl.pallas_call(
        functools.partial(_attn_prompt_kernel, lam_init=lam_init),
        grid_spec=grid_spec,
        out_shape=jax.ShapeDtypeStruct((batch * seq, D_MODEL), F32),
        compiler_params=_cparams(3),
    )(scal, q, k, v, g)


NEW_TOKENS_PAD = LANES // DA_HEADS


def _attn_sample_kernel(pt_ref, scal_ref, q_ref, *rest, dec_seq, page, lam_init):
    k_refs, v_refs = rest[:SAMPLE_PAGES], rest[SAMPLE_PAGES:2 * SAMPLE_PAGES]
    (kn_ref, vn_ref, bias_ref, biasn_ref, g_ref, o_ref, qs_ref, m_ref, l_ref, acc_ref
     ) = rest[2 * SAMPLE_PAGES:]
    step = pl.program_id(1)
    rows = 2 * dec_seq * DA_HEADS
    row_h = lax.broadcasted_iota(jnp.int32, (rows, 1), 0) % DA_HEADS
    slope = jnp.zeros((rows, 1), F32)
    for h in range(DA_HEADS):
        slope = jnp.where(row_h == h, scal_ref[h], slope)

    @pl.when(step == 0)
    def _():
        q = q_ref[0] * (DA_HEAD ** -0.5)
        lane = lax.broadcasted_iota(jnp.int32, q.shape, 1)
        qs_ref[...] = jnp.concatenate(
            [jnp.where(lane < DA_HEAD, q, 0.0), jnp.where(lane >= DA_HEAD, q, 0.0)],
            axis=0).astype(BF16)
        m_ref[...] = jnp.full_like(m_ref, -jnp.inf)
        l_ref[...] = jnp.zeros_like(l_ref)
        acc_ref[...] = jnp.zeros_like(acc_ref)

    def update(scores, values):
        m_prev = m_ref[...]
        m_new = functools.reduce(jnp.maximum,
                                 [jnp.max(s, axis=-1, keepdims=True) for s in scores], m_prev)
        alpha = jnp.exp(m_prev - m_new)
        l_new = alpha * l_ref[...]
        acc = alpha * acc_ref[...]
        for s, vb in zip(scores, values):
            pr = jnp.exp(s - m_new)
            l_new = l_new + jnp.sum(pr, axis=-1, keepdims=True)
            acc = acc + _dot(pr.astype(BF16), vb)
        m_ref[...] = m_new
        l_ref[...] = l_new
        acc_ref[...] = acc

    flat = lambda ref: ref[0].reshape(page * DA_HEADS, DA_VDIM).astype(BF16)
    scores = []
    for j in range(SAMPLE_PAGES):
        first_tok = ((step * SAMPLE_PAGES + j) * page).astype(F32)
        scores.append(_dot_nt(qs_ref[...], flat(k_refs[j])) - bias_ref[...] + slope * first_tok)
    update(scores, [flat(v_ref) for v_ref in v_refs])

    @pl.when(step == pl.num_programs(1) - 1)
    def _():
        update([_dot_nt(qs_ref[...], kn_ref[0].astype(BF16)) - biasn_ref[...]],
               [vn_ref[0].astype(BF16)])
        o = acc_ref[...] / l_ref[...]
        of = o[:rows // 2] - scal_ref[DA_HEADS] * o[rows // 2:]
        o_ref[0] = _subln(of, g_ref[...], lam_init)


def _attn_sample(page_table, scal, q, cache_k, cache_v, k_new, v_new, g, lam_init):
    b = q.shape[0]
    dec_seq = q.shape[1] // DA_HEADS
    n_pages = page_table.shape[1]
    page = cache_k.shape[1]
    past_len = n_pages * page
    rows = 2 * dec_seq * DA_HEADS
    assert n_pages % SAMPLE_PAGES == 0 and dec_seq <= NEW_TOKENS_PAD

    slopes = scal[:DA_HEADS]
    r = jnp.arange(rows)
    r_q, r_h = (r // DA_HEADS) % dec_seq, r % DA_HEADS

    def bias_tile(n_tok, first_q_pos, causal):
        col = jnp.arange(n_tok * DA_HEADS)
        dist = first_q_pos + r_q[:, None] - (col // DA_HEADS)[None, :]
        ok = (col % DA_HEADS)[None, :] == r_h[:, None]
        if causal:
            ok = ok & (dist >= 0)
        return jnp.where(ok, slopes[r_h][:, None] * dist.astype(F32), -NEG_INF)

    bias_past = bias_tile(page, past_len, False)
    bias_new = bias_tile(NEW_TOKENS_PAD, 0, True)

    def page_spec(j):
        return pl.BlockSpec((1, page, DA_HEADS, DA_VDIM),
                            lambda i, p, pt, s: (pt[i * n_pages + p * SAMPLE_PAGES + j], 0, 0, 0))

    per_seq = lambda shape: pl.BlockSpec((1,) + shape, lambda i, p, pt, s: (i, 0, 0))
    const = lambda shape: pl.BlockSpec(shape, lambda i, p, pt, s: (0, 0))
    grid_spec = pltpu.PrefetchScalarGridSpec(
        num_scalar_prefetch=2,
        grid=(b, n_pages // SAMPLE_PAGES),
        in_specs=[per_seq((dec_seq * DA_HEADS, DA_VDIM))]
                 + [page_spec(j) for j in range(SAMPLE_PAGES)] * 2
                 + [per_seq((LANES, DA_VDIM)), per_seq((LANES, DA_VDIM)),
                    const(bias_past.shape), const(bias_new.shape), const((1, DA_VDIM))],
        out_specs=per_seq((dec_seq * DA_HEADS, DA_VDIM)),
        scratch_shapes=[pltpu.VMEM((rows, DA_VDIM), BF16),
                        pltpu.VMEM((rows, 1), F32), pltpu.VMEM((rows, 1), F32),
                        pltpu.VMEM((rows, DA_VDIM), F32)])
    return pl.pallas_call(
        functools.partial(_attn_sample_kernel, dec_seq=dec_seq, page=page, lam_init=lam_init),
        grid_spec=grid_spec,
        out_shape=jax.ShapeDtypeStruct(q.shape, F32),
        compiler_params=_cparams(2),
    )(page_table.reshape(-1), scal, q, *([cache_k] * SAMPLE_PAGES), *([cache_v] * SAMPLE_PAGES),
      k_new, v_new, bias_past, bias_new, g)


def _lambda_init(layer):
    return 0.8 - 0.6 * math.exp(-0.3 * layer)


def _peer_weights(layer, peer_w_q, peer_b_q, peer_subkeys, peer_u, peer_v, ln2_g, ln2_b):
    wq_t = peer_w_q[layer].T.astype(BF16)
    bq_b = jnp.broadcast_to(peer_b_q[layer][:, None], (PEER_HEADS * PEER_QDIM, ROUTE_TILE))
    return (wq_t, bq_b, peer_subkeys[layer].astype(BF16), peer_u[layer].astype(BF16),
            peer_v[layer].astype(BF16).T, jnp.stack([ln2_g[layer], ln2_b[layer]]))


def _peer_layer(x, wq_t, bq_b, subkeys, u_bf, vt_bf, ln_vec):
    xt, rk, e2, e1, nsel = _peer_route(x, wq_t, bq_b, subkeys)
    return _peer_sweep(x, xt, rk, e2, e1, nsel, u_bf, vt_bf, ln_vec)


def kernel(x_prompt, x_sample, state_shift, state_wkv, cache_k, cache_v, page_table, p_prompt, p_sample, rw_mu, rw_w_r, rw_w_k, rw_w_v, rw_w0, rw_w1, rw_w2, rw_a0, rw_a1, rw_a2, rw_g1, rw_g2, rw_k_k, rw_k_a, rw_r_k, rw_gn_g, rw_gn_b, rw_w_o, da_w_k, da_w_v, da_w_q, da_lam, da_subln_g, da_w_o, ln1_g, ln1_b, ln2_g, ln2_b, peer_w_q, peer_b_q, peer_subkeys, peer_u, peer_v, ple_w_p, ple_w_g, ple_b_g):
    bp, tp, d = x_prompt.shape
    bs, ts, _ = x_sample.shape
    assert d == D_MODEL and bp * RW_HEADS == LANES and bs == LANES
    np_, ns = bp * tp, bs * ts
    hh, hd = RW_HEADS, RW_HEAD
    bf = lambda w: w.astype(BF16)

    head_of_lane = jnp.arange(d) // hd
    ind = (head_of_lane[:, None] == jnp.arange(LANES)[None, :]).astype(BF16)
    ind_t = ind.T
    pre_consts = (rw_mu[0], bf(rw_w_r[0]), bf(rw_w_k[0]), bf(rw_w_v[0]), bf(rw_w1[0]), bf(rw_w2[0]),
                  bf(rw_a1[0]), bf(rw_a2[0]), bf(rw_g1[0]), bf(rw_g2[0]),
                  jnp.stack([rw_w0[0], rw_a0[0], rw_k_k[0], rw_k_a[0]]), ind, ind_t)
    post_consts = (jnp.stack([rw_gn_g[0], rw_gn_b[0], rw_r_k[0].reshape(d), ln1_g[0], ln1_b[0]]),
                   ind, ind_t, bf(rw_w_o[0]))
    peer_consts = [_peer_weights(layer, peer_w_q, peer_b_q, peer_subkeys, peer_u, peer_v, ln2_g, ln2_b)
                   for layer in range(DEPTH)]
    ple_consts = [(bf(ple_w_p[layer]), bf(ple_w_g[layer]), ple_b_g[layer][None])
                  for layer in range(DEPTH)]
    kvq_w = (bf(da_w_k), bf(da_w_v), bf(da_w_q[0]))
    attn_out = (jnp.stack([ln1_g[1], ln1_b[1]]), bf(da_w_o[0]))
    lam_init = _lambda_init(1)
    lv = da_lam[0].astype(F32)
    lam = jnp.exp(jnp.sum(lv[0] * lv[1])) - jnp.exp(jnp.sum(lv[2] * lv[3])) + lam_init
    slopes = 2.0 ** (-8.0 * jnp.arange(1, DA_HEADS + 1, dtype=F32) / DA_HEADS)
    scal = jnp.concatenate([slopes, lam[None]]).astype(F32)
    subln_g = da_subln_g[0][None]

    def trunk(mixer, p, attend):
        x, state = mixer()
        x = _peer_layer(x, *peer_consts[0])
        x, k_sh, v_sh, q = _ple(x, p[0], *ple_consts[0], kvq_w)
        x = _proj_ln(x, attend(q, k_sh, v_sh), *attn_out)
        x = _peer_layer(x, *peer_consts[1])
        (x,) = _ple(x, p[1], *ple_consts[1])
        return x, state, k_sh, v_sh

    def prompt_mixer():
        r, w, k, v, a, b, g = _rwkv_pre_tm(x_prompt, jnp.zeros((bp, d), F32), *pre_consts)
        y, state = _wkv_scan_tm(r, w, k, v, a, b, bp)
        return _rwkv_post_tm(x_prompt, y, r, k, v, g, *post_consts).reshape(np_, d), state

    y_prompt, st_p, k_p, v_p = trunk(
        prompt_mixer, p_prompt.reshape(DEPTH, np_, -1),
        lambda q, k_sh, v_sh: _attn_prompt(scal, q, k_sh, v_sh, subln_g, bp, tp, lam_init))

    def sample_mixer():
        x = x_sample.reshape(ns, d)
        x_prev = jnp.concatenate([state_shift[0][:, None, :], x_sample[:, :-1]], axis=1).reshape(ns, d)
        r, w, k, v, a, b, g = _rwkv_pre(x, x_prev, *pre_consts)
        to_chains = lambda t: t.reshape(bs, ts, hh, hd).transpose(2, 1, 3, 0)
        y, state = _wkv_scan(*(to_chains(t) for t in (r, w, k, v, a, b)),
                             state_wkv[0].astype(F32).transpose(1, 2, 3, 0))
        y = y.transpose(3, 1, 0, 2).reshape(ns, d)
        return _rwkv_post(x, y, r, k, v, g, *post_consts), state

    new_rows = lambda t: jnp.pad(t.reshape(bs, ts, d), ((0, 0), (0, NEW_TOKENS_PAD - ts), (0, 0))
                                 ).reshape(bs, LANES, DA_VDIM)
    y_sample, st_s, k_s, v_s = trunk(
        sample_mixer, p_sample.reshape(DEPTH, ns, -1),
        lambda q, k_sh, v_sh: _attn_sample(
            page_table, scal, q.reshape(bs, ts * DA_HEADS, DA_VDIM), cache_k, cache_v,
            new_rows(k_sh), new_rows(v_sh), subln_g, lam_init).reshape(ns, d))

    wkv_p = st_p.reshape(hd, hd, 2, hh // 2, bp).transpose(4, 3, 2, 0, 1).reshape(bp, hh, hd, hd)
    wkv_p = wkv_p[None].astype(state_wkv.dtype)
    wkv_s = st_s.transpose(3, 0, 1, 2)[None].astype(state_wkv.dtype)
    kv_shape_p = (bp, tp, DA_HEADS, DA_VDIM)
    kv_shape_s = (bs, ts, DA_HEADS, DA_VDIM)
    return (y_prompt.reshape(bp, tp, d), y_sample.reshape(bs, ts, d),
            x_prompt[:, -1][None], wkv_p, k_p.reshape(kv_shape_p), v_p.reshape(kv_shape_p),
            x_sample[:, -1][None], wkv_s, k_s.reshape(kv_shape_s), v_s.reshape(kv_shape_s))
```

```python
import functools
import math

import jax
import jax.numpy as jnp
from jax import lax
from jax.experimental import pallas as pl
from jax.experimental.pallas import tpu as pltpu

F32 = jnp.float32
BF16 = jnp.bfloat16

D_MODEL = 1024
DEPTH = 2
RW_HEAD = 64
RW_HEADS = D_MODEL // RW_HEAD
RW_GN_EPS = 64e-5
DA_HEAD = 64
DA_HEADS = D_MODEL // (2 * DA_HEAD)
DA_VDIM = 2 * DA_HEAD
RMS_EPS = 1e-5
NEG_INF = -1e30
PEER_HEADS = 8
PEER_NKEYS = 128
PEER_NEXPERTS = PEER_NKEYS * PEER_NKEYS
PEER_QDIM = 256
PEER_HALF = PEER_QDIM // 2
PEER_TOPK = 16
DN_ALPHA = (2.0 * DEPTH) ** 0.25
LN_EPS = 1e-5

LANES = 128
BF16_ROWS = 16
VMEM_LIMIT = 56 * 1024 * 1024

ROW_TILE = 256
ROUTE_TILE = 256
PEER_TOK_TILE = 512
PEER_EXP_TILE = 2048
SCAN_TCHUNK = 32
SCAN_VROWS = 4
SCAN_UNROLL = 8
ATT_TILE = 512
SAMPLE_PAGES = 16


def _cparams(n_axes):
    return pltpu.CompilerParams(dimension_semantics=("arbitrary",) * n_axes,
                                vmem_limit_bytes=VMEM_LIMIT)


def _dot(a, b):
    return jnp.dot(a, b, preferred_element_type=F32)


def _dot_nt(a, b):
    return lax.dot_general(a, b, (((1,), (1,)), ((), ())), preferred_element_type=F32)


def _layer_norm(x, g, b):
    mu = jnp.mean(x, axis=-1, keepdims=True)
    xc = x - mu
    var = jnp.mean(xc * xc, axis=-1, keepdims=True)
    return xc * lax.rsqrt(var + LN_EPS) * g + b


def _split2(x):
    hi = x.astype(BF16)
    lo = (x - hi.astype(F32)).astype(BF16)
    return hi, lo


def _head_sum(x, ind, ind_t):
    hi, lo = _split2(x)
    s = _dot(hi, ind) + _dot(lo, ind)
    shi, slo = _split2(s)
    return _dot(shi, ind_t) + _dot(slo, ind_t)


def _full(shape):
    n = len(shape)
    return pl.BlockSpec(shape, lambda *_: (0,) * n)


def _rwkv_pre_kernel(x_ref, xp_ref, *refs):
    _rwkv_pre_math(x_ref[...], xp_ref[...], *refs)


def _batch_to_time_major(ref, scratch):
    nb, steps, dm = ref.shape
    tiles = range(dm // LANES)
    for b in range(nb):
        blk = ref[b]
        for c in tiles:
            scratch[c, b * steps:(b + 1) * steps, :] = blk[:, c * LANES:(c + 1) * LANES]
    return jnp.concatenate(
        [jnp.concatenate([scratch[c, pl.ds(t, nb, stride=steps), :] for c in tiles], axis=1)
         for t in range(steps)], axis=0)


def _time_to_batch_major(val, ref, scratch):
    nb, steps, dm = ref.shape
    tiles = range(dm // LANES)
    for t in range(steps):
        for c in tiles:
            scratch[c, pl.ds(t, nb, stride=steps), :] = val[t * nb:(t + 1) * nb, c * LANES:(c + 1) * LANES]
    for b in range(nb):
        ref[b] = jnp.concatenate([scratch[c, b * steps:(b + 1) * steps, :] for c in tiles], axis=1)


def _rwkv_pre_tm_kernel(x_ref, xl_ref, shift_ref, *refs):
    *refs, flat_ref = refs
    nb = x_ref.shape[0]
    x = _batch_to_time_major(x_ref, flat_ref)
    last = jnp.concatenate([xl_ref[b, xl_ref.shape[1] - 1:, :] for b in range(nb)], axis=0)
    first = jnp.where(pl.program_id(0) == 0, shift_ref[...], last)
    _rwkv_pre_math(x, jnp.concatenate([first, x[:-nb]], axis=0), *refs)


def _rwkv_pre_math(x, xp, mu_ref, wr_ref, wk_ref, wv_ref, w1_ref, w2_ref, a1_ref, a2_ref,
                   g1_ref, g2_ref, vec_ref, ind_ref, indt_ref,
                   r_ref, w_ref, k_ref, v_ref, a_ref, b_ref, g_ref):
    xx = xp - x
    mix = lambda i: (x + xx * mu_ref[i:i + 1, :]).astype(BF16)
    xr, xw, xk, xv, xa, xg = (mix(i) for i in range(6))
    w0, a0, k_k, k_a = (vec_ref[i:i + 1, :] for i in range(4))
    r = _dot(xr, wr_ref[...])
    k = _dot(xk, wk_ref[...])
    v = _dot(xv, wv_ref[...])
    wl = w0 + _dot(jnp.tanh(_dot(xw, w1_ref[...])).astype(BF16), w2_ref[...])
    z = -wl
    softplus = jnp.maximum(z, 0.0) + jnp.log(1.0 + jnp.exp(-jnp.abs(z)))
    decay = jnp.exp(-jnp.exp(-softplus - 0.5))
    a = jax.nn.sigmoid(a0 + _dot(_dot(xa, a1_ref[...]).astype(BF16), a2_ref[...]))
    g = _dot(jax.nn.sigmoid(_dot(xg, g1_ref[...])).astype(BF16), g2_ref[...])
    kk = k * k_k
    n2 = _head_sum(kk * kk, ind_ref[...], indt_ref[...])
    kk = kk / jnp.maximum(jnp.sqrt(n2), 1e-12)
    r_ref[...] = r
    w_ref[...] = decay
    k_ref[...] = k * (1.0 + (a - 1.0) * k_a)
    v_ref[...] = v
    a_ref[...] = -kk
    b_ref[...] = kk * a
    g_ref[...] = g


def _rwkv_pre(x, xp, mu, w_r, w_k, w_v, w1, w2, a1, a2, g1, g2, vec, ind, ind_t):
    n = x.shape[0]
    tm = ROW_TILE
    row = pl.BlockSpec((tm, D_MODEL), lambda i: (i, 0))
    consts = (mu, w_r, w_k, w_v, w1, w2, a1, a2, g1, g2, vec, ind, ind_t)
    return pl.pallas_call(
        _rwkv_pre_kernel,
        grid=(n // tm,),
        in_specs=[row, row] + [_full(c.shape) for c in consts],
        out_specs=[row] * 7,
        out_shape=[jax.ShapeDtypeStruct((n, D_MODEL), F32)] * 7,
        compiler_params=_cparams(1),
    )(x, xp, *consts)


TM_LAST = 8


def _rwkv_pre_tm(x, shift, *consts):
    nb, t, _ = x.shape
    steps = ROW_TILE // nb
    row = pl.BlockSpec((ROW_TILE, D_MODEL), lambda i: (i, 0))
    return pl.pallas_call(
        _rwkv_pre_tm_kernel,
        grid=(t // steps,),
        in_specs=[pl.BlockSpec((nb, steps, D_MODEL), lambda i: (0, i, 0)),
                  pl.BlockSpec((nb, TM_LAST, D_MODEL),
                               lambda i: (0, jnp.maximum(i * (steps // TM_LAST) - 1, 0), 0)),
                  _full(shift.shape)] + [_full(c.shape) for c in consts],
        out_specs=[row] * 7,
        out_shape=[jax.ShapeDtypeStruct((t * nb, D_MODEL), F32)] * 7,
        scratch_shapes=[pltpu.VMEM((D_MODEL // LANES, ROW_TILE, LANES), F32)],
        compiler_params=_cparams(1),
    )(x, x, shift, *consts)


def _scan_chunk(r_ref, w_ref, k_ref, v_ref, a_ref, b_ref, y_ref, st_ref, wr_ref, dot_ref, tchunk):
    def prep(t, carry):
        r = r_ref[t]
        wr_ref[t] = w_ref[t] * r
        dot_ref[t, 0:1, :] = jnp.sum(b_ref[t] * r, axis=0, keepdims=True)
        dot_ref[t, 1:2, :] = jnp.sum(k_ref[t] * r, axis=0, keepdims=True)
        return carry

    lax.fori_loop(0, tchunk, prep, 0)

    def vloop(vc, carry):
        v0 = vc * SCAN_VROWS
        state = tuple(st_ref[v0 + j] for j in range(SCAN_VROWS))

        def step(t, st):
            a = a_ref[t]
            wr = wr_ref[t]
            w = w_ref[t]
            b = b_ref[t]
            k = k_ref[t]
            br = dot_ref[t, 0:1, :]
            kr = dot_ref[t, 1:2, :]
            out = []
            for j in range(SCAN_VROWS):
                s = st[j]
                val = v_ref[t, pl.ds(v0 + j, 1), :]
                sa = jnp.sum(s * a, axis=0, keepdims=True)
                y0 = jnp.sum(s * wr, axis=0, keepdims=True)
                y_ref[t, pl.ds(v0 + j, 1), :] = y0 + sa * br + val * kr
                out.append(s * w + sa * b + val * k)
            return tuple(out)

        state = lax.fori_loop(0, tchunk, step, state, unroll=SCAN_UNROLL)
        for j in range(SCAN_VROWS):
            st_ref[v0 + j] = state[j]
        return carry

    lax.fori_loop(0, RW_HEAD // SCAN_VROWS, vloop, 0)


def _scan_kernel(r_ref, w_ref, k_ref, v_ref, a_ref, b_ref, s0_ref, y_ref, st_ref, wr_ref, dot_ref,
                 *, tchunk):
    @pl.when(pl.program_id(1) == 0)
    def _():
        st_ref[...] = s0_ref[...]

    _scan_chunk(*(ref.at[0] for ref in (r_ref, w_ref, k_ref, v_ref, a_ref, b_ref, y_ref, st_ref)),
                wr_ref, dot_ref, tchunk)


def _slabs_to_tiles(slab0, slab1):
    stack = lambda slab: [slab[:, g * LANES:(g + 1) * LANES] for g in range(slab.shape[1] // LANES)]
    t = jnp.concatenate(stack(slab0) + stack(slab1), axis=0).T
    even, odd = t[:RW_HEAD], t[RW_HEAD:]
    low = lax.broadcasted_iota(jnp.int32, even.shape, 1) < RW_HEAD
    return (jnp.where(low, even, pltpu.roll(odd, RW_HEAD, axis=1)),
            jnp.where(low, pltpu.roll(even, RW_HEAD, axis=1), odd))


def _tiles_to_slabs(tile0, tile1, nb):
    low = lax.broadcasted_iota(jnp.int32, tile0.shape, 1) < RW_HEAD
    even = jnp.where(low, tile0, pltpu.roll(tile1, RW_HEAD, axis=1))
    odd = jnp.where(low, pltpu.roll(tile0, RW_HEAD, axis=1), tile1)
    t = jnp.concatenate([even, odd], axis=0).T
    slab = lambda rows: jnp.concatenate([rows[g * nb:(g + 1) * nb] for g in range(RW_HEAD // nb)], axis=1)
    return slab(t[:RW_HEAD]), slab(t[RW_HEAD:])


def _scan_tm_kernel(r_ref, w_ref, k_ref, v_ref, a_ref, b_ref, y_ref, st_ref,
                    ops_ref, ych_ref, wr_ref, dot_ref, *, tchunk, nb):
    @pl.when(pl.program_id(0) == 0)
    def _():
        st_ref[...] = jnp.zeros_like(st_ref)

    rows = lambda t: pl.ds(pl.multiple_of(t * nb, nb), nb)

    def load(p, carry):
        t = 2 * p
        for i, ref in enumerate((r_ref, w_ref, k_ref, v_ref, a_ref, b_ref)):
            ops_ref[i, t], ops_ref[i, t + 1] = _slabs_to_tiles(ref[rows(t), :], ref[rows(t + 1), :])
        return carry

    lax.fori_loop(0, tchunk // 2, load, 0, unroll=2)
    _scan_chunk(*(ops_ref.at[i] for i in range(6)), ych_ref, st_ref, wr_ref, dot_ref, tchunk)

    def store(p, carry):
        t = 2 * p
        y_ref[rows(t), :], y_ref[rows(t + 1), :] = _tiles_to_slabs(ych_ref[t], ych_ref[t + 1], nb)
        return carry

    lax.fori_loop(0, tchunk // 2, store, 0, unroll=2)


def _wkv_scan_tm(r, w, k, v, a, b, nb):
    assert nb * RW_HEADS == LANES and nb == SUBLANES
    t = r.shape[0] // nb
    tchunk = SCAN_TCHUNK
    seq = pl.BlockSpec((tchunk * nb, D_MODEL), lambda ti: (ti, 0))
    st = pl.BlockSpec((RW_HEAD, RW_HEAD, LANES), lambda ti: (0, 0, 0))
    tile = (tchunk, RW_HEAD, LANES)
    return pl.pallas_call(
        functools.partial(_scan_tm_kernel, tchunk=tchunk, nb=nb),
        grid=(t // tchunk,),
        in_specs=[seq] * 6,
        out_specs=[seq, st],
        out_shape=[jax.ShapeDtypeStruct(r.shape, F32),
                   jax.ShapeDtypeStruct((RW_HEAD, RW_HEAD, LANES), F32)],
        scratch_shapes=[pltpu.VMEM((6,) + tile, F32), pltpu.VMEM(tile, F32), pltpu.VMEM(tile, F32),
                        pltpu.VMEM((tchunk, 8, LANES), F32)],
        compiler_params=_cparams(1),
    )(r, w, k, v, a, b)


def _wkv_scan(r, w, k, v, a, b, s0):
    g, t = r.shape[0], r.shape[1]
    tchunk = min(SCAN_TCHUNK, t)
    seq = pl.BlockSpec((1, tchunk, RW_HEAD, LANES), lambda gi, ti: (gi, ti, 0, 0))
    st = pl.BlockSpec((1, RW_HEAD, RW_HEAD, LANES), lambda gi, ti: (gi, 0, 0, 0))
    return pl.pallas_call(
        functools.partial(_scan_kernel, tchunk=tchunk),
        grid=(g, t // tchunk),
        in_specs=[seq] * 6 + [st],
        out_specs=[seq, st],
        out_shape=[jax.ShapeDtypeStruct(r.shape, F32), jax.ShapeDtypeStruct(s0.shape, F32)],
        scratch_shapes=[pltpu.VMEM((tchunk, RW_HEAD, LANES), F32), pltpu.VMEM((tchunk, 8, LANES), F32)],
        compiler_params=_cparams(2),
    )(r, w, k, v, a, b, s0)


def _rwkv_post_math(x, y_ref, r_ref, k_ref, v_ref, g_ref, vec_ref, ind_ref, indt_ref, wo_ref):
    ind, ind_t = ind_ref[...], indt_ref[...]
    gn_g, gn_b, r_k, ln_g, ln_b = (vec_ref[i:i + 1, :] for i in range(5))
    y = y_ref[...]
    inv = 1.0 / RW_HEAD
    yc = y - _head_sum(y, ind, ind_t) * inv
    var = _head_sum(yc * yc, ind, ind_t) * inv
    yn = yc * lax.rsqrt(var + RW_GN_EPS) * gn_g + gn_b
    v = v_ref[...]
    bonus = _head_sum(r_ref[...] * k_ref[...] * r_k, ind, ind_t) * v
    h = _dot(((yn + bonus) * g_ref[...]).astype(BF16), wo_ref[...])
    return _layer_norm(DN_ALPHA * x + h, ln_g, ln_b)


def _rwkv_post_kernel(x_ref, *refs):
    *refs, o_ref = refs
    o_ref[...] = _rwkv_post_math(x_ref[...], *refs)


def _rwkv_post_tm_kernel(x_ref, *refs):
    *refs, o_ref, flat_ref = refs
    nb, steps, _ = x_ref.shape
    out = _rwkv_post_math(_batch_to_time_major(x_ref, flat_ref), *refs)
    _time_to_batch_major(out, o_ref, flat_ref)


def _rwkv_post_tm(x, y, r, k, v, g, *consts):
    nb, t, _ = x.shape
    steps = ROW_TILE // nb
    row = pl.BlockSpec((ROW_TILE, D_MODEL), lambda i: (i, 0))
    blk = pl.BlockSpec((nb, steps, D_MODEL), lambda i: (0, i, 0))
    return pl.pallas_call(
        _rwkv_post_tm_kernel,
        grid=(t // steps,),
        in_specs=[blk] + [row] * 5 + [_full(c.shape) for c in consts],
        out_specs=blk,
        out_shape=jax.ShapeDtypeStruct(x.shape, F32),
        scratch_shapes=[pltpu.VMEM((D_MODEL // LANES, ROW_TILE, LANES), F32)],
        compiler_params=_cparams(1),
    )(x, y, r, k, v, g, *consts)


def _rwkv_post(x, y, r, k, v, g, vec, ind, ind_t, w_o):
    n = x.shape[0]
    tm = ROW_TILE
    row = pl.BlockSpec((tm, D_MODEL), lambda i: (i, 0))
    consts = (vec, ind, ind_t, w_o)
    return pl.pallas_call(
        _rwkv_post_kernel,
        grid=(n // tm,),
        in_specs=[row] * 6 + [_full(c.shape) for c in consts],
        out_specs=row,
        out_shape=jax.ShapeDtypeStruct((n, D_MODEL), F32),
        compiler_params=_cparams(1),
    )(x, y, r, k, v, g, *consts)


def _proj_ln_kernel(x_ref, h_ref, vec_ref, wo_ref, o_ref):
    h = _dot(h_ref[...].astype(BF16), wo_ref[...])
    o_ref[...] = _layer_norm(DN_ALPHA * x_ref[...] + h, vec_ref[0:1, :], vec_ref[1:2, :])


def _proj_ln(x, h, vec, w_o):
    n = x.shape[0]
    tm = ROW_TILE
    row = pl.BlockSpec((tm, D_MODEL), lambda i: (i, 0))
    return pl.pallas_call(
        _proj_ln_kernel,
        grid=(n // tm,),
        in_specs=[row, row, _full(vec.shape), _full(w_o.shape)],
        out_specs=row,
        out_shape=jax.ShapeDtypeStruct((n, D_MODEL), F32),
        compiler_params=_cparams(1),
    )(x, h, vec, w_o)


SUBLANES = 8


def _sort_network(n):
    def merge(lo, hi, r):
        step = r * 2
        if step < hi - lo:
            yield from merge(lo, hi, step)
            yield from merge(lo + r, hi, step)
            yield from [(i, i + r) for i in range(lo + r, hi - r, step)]
        else:
            yield (lo, lo + r)

    def sort(lo, hi):
        if hi - lo >= 1:
            mid = lo + (hi - lo) // 2
            yield from sort(lo, mid)
            yield from sort(mid + 1, hi)
            yield from merge(lo, hi, 1)

    return list(sort(0, n - 1))


def _bf16_pair_words(x):
    bits = lax.bitcast_convert_type(x.astype(BF16).astype(F32), jnp.int32)
    return bits | lax.shift_right_logical(bits, jnp.full_like(bits, 16))


def _top_list(s, top_ref, cnt_ref, c):
    stack = [s[g * SUBLANES:(g + 1) * SUBLANES, :] for g in range(s.shape[0] // SUBLANES)]
    for i, j in _sort_network(len(stack)):
        stack[i], stack[j] = jnp.maximum(stack[i], stack[j]), jnp.minimum(stack[i], stack[j])
    for j in range(PEER_TOPK):
        m = jnp.max(stack[0], axis=0, keepdims=True)
        hit = stack[0] >= m
        top_ref[c, j:j + 1, :] = m
        cnt_ref[c, j:j + 1, :] = jnp.sum(jnp.where(hit, 1.0, 0.0), axis=0, keepdims=True)
        for r in range(PEER_TOPK - 1 - j):
            stack[r] = jnp.where(hit, stack[r + 1], stack[r])


def _peer_route_kernel(x_ref, wq_ref, bq_ref, sk_ref, xt_ref, rk_ref, e2_ref, e1_ref, ns_ref,
                       top_ref, cnt_ref):
    xt = x_ref[...].T.astype(BF16)
    xt_ref[...] = xt
    q = _dot(wq_ref[...], xt) + bq_ref[...]
    k = PEER_TOPK
    assert PEER_NKEYS // SUBLANES >= k and k == 2 * SUBLANES
    for h in range(PEER_HEADS):
        sc = []
        for c in range(2):
            lo = (2 * h + c) * PEER_HALF
            s = _dot(sk_ref[c], q[lo:lo + PEER_HALF, :].astype(BF16))
            sc.append(s)
            _top_list(s, top_ref, cnt_ref, c)
        ta, tb = top_ref[0], top_ref[1]
        ca, cb = cnt_ref[0], cnt_ref[1]
        lo_a, hi_a = ta[:SUBLANES], ta[SUBLANES:]
        cand = [lo_a + tb[j:j + 1, :] for j in range(k)]
        mult = [ca[:SUBLANES] * cb[j:j + 1, :] for j in range(k)]
        cand_hi = hi_a + tb[0:1, :]
        mult_hi = ca[SUBLANES:] * cb[0:1, :]
        best = thr = zsum = None
        remaining = jnp.full((1, xt.shape[1]), float(k), F32)
        for j in range(k):
            m = jnp.max(jnp.maximum(cand[0], cand_hi), axis=0, keepdims=True)
            hit, hit_hi = cand[0] >= m, cand_hi >= m
            n_hit = jnp.where(hit, mult[0], 0.0) + jnp.where(hit_hi, mult_hi, 0.0)
            take = jnp.minimum(jnp.sum(n_hit, axis=0, keepdims=True), remaining)
            if j == 0:
                best, thr, zsum = m, m, take
            else:
                zsum = zsum + take * jnp.exp(m - best)
                thr = jnp.where(take > 0.0, m, thr)
            remaining = remaining - take
            for r in range(k - 1 - j):
                cand[r] = jnp.where(hit, cand[r + 1], cand[r])
                mult[r] = jnp.where(hit, mult[r + 1], mult[r])
            cand_hi = jnp.where(hit_hi, -jnp.inf, cand_hi)
        s1, s2 = sc
        n_slot = jnp.zeros_like(ta)
        for j in range(k):
            n_slot = n_slot + jnp.where(ta + tb[j:j + 1, :] >= thr, 1.0, 0.0)
        nsel = jnp.zeros_like(s1)
        rank = jnp.full_like(s2, float(k))
        for j in reversed(range(k)):
            nsel = jnp.where(s1 >= ta[j:j + 1, :], n_slot[j:j + 1, :], nsel)
            rank = jnp.where(s2 >= tb[j:j + 1, :], float(j), rank)
        rk_ref[h] = rank.astype(BF16)
        ns_ref[h] = _bf16_pair_words(nsel)
        e1_ref[h] = _bf16_pair_words(jnp.exp(s1 - ta[0:1, :]))
        e2_ref[h] = (jnp.exp(s2 - tb[0:1, :]) / zsum).astype(BF16)


def _peer_route(x, wq_t, bq_b, subkeys):
    n = x.shape[0]
    tb = ROUTE_TILE
    tab = pl.BlockSpec((PEER_HEADS, PEER_NKEYS, tb), lambda i: (0, 0, i))
    tab_words = jax.ShapeDtypeStruct((PEER_HEADS, PEER_NKEYS, n), jnp.int32)
    tab_bf16 = jax.ShapeDtypeStruct((PEER_HEADS, PEER_NKEYS, n), BF16)
    return pl.pallas_call(
        _peer_route_kernel,
        grid=(n // tb,),
        in_specs=[pl.BlockSpec((tb, D_MODEL), lambda i: (i, 0)),
                  _full(wq_t.shape), _full(bq_b.shape), _full(subkeys.shape)],
        out_specs=[pl.BlockSpec((D_MODEL, tb), lambda i: (0, i)), tab, tab, tab, tab],
        out_shape=[jax.ShapeDtypeStruct((D_MODEL, n), BF16), tab_bf16, tab_bf16, tab_words, tab_words],
        scratch_shapes=[pltpu.VMEM((2, PEER_TOPK, tb), F32), pltpu.VMEM((2, PEER_TOPK, tb), F32)],
        compiler_params=_cparams(1),
    )(x, wq_t, bq_b, subkeys)


def _gelu_tanh(x):
    c0 = math.sqrt(2.0 / math.pi)
    inner = x * (jnp.asarray(c0, x.dtype) + jnp.asarray(c0 * 0.044715, x.dtype) * (x * x))
    hx = jnp.asarray(0.5, x.dtype) * x
    return hx + hx * jnp.tanh(inner)


def _peer_sweep_kernel(x_ref, xt_ref, rk_ref, e2_ref, e1_ref, ns_ref, u_ref, vt_ref, vec_ref,
                       o_ref, acc_ref):
    e = pl.program_id(1)

    @pl.when(e == 0)
    def _():
        acc_ref[...] = jnp.zeros_like(acc_ref)

    n_sub = PEER_EXP_TILE // PEER_NKEYS
    tb = xt_ref.shape[1]
    zero = jnp.zeros((), BF16)
    first = pl.multiple_of(e * n_sub, n_sub)

    def row_tile(ref, h, j):
        row = ref[h, pl.ds(first, n_sub), :][j:j + 1, :]
        tile = pltpu.bitcast(jnp.broadcast_to(row, (BF16_ROWS // 2, tb)), BF16)
        return jnp.concatenate([tile] * (PEER_NKEYS // BF16_ROWS), axis=0)

    acts = []
    for j in range(n_sub):
        rows = slice(j * PEER_NKEYS, (j + 1) * PEER_NKEYS)
        gate = None
        for h in range(PEER_HEADS):
            keep = rk_ref[h] < row_tile(ns_ref, h, j)
            term = jnp.where(keep, e2_ref[h], zero) * row_tile(e1_ref, h, j)
            gate = term if gate is None else gate + term
        ht = _dot(u_ref[rows, :], xt_ref[...])
        acts.append(gate * _gelu_tanh(ht.astype(BF16)))
    acc_ref[...] += _dot(vt_ref[...], jnp.concatenate(acts, axis=0))

    @pl.when(e == pl.num_programs(1) - 1)
    def _():
        c = acc_ref[...].T
        o_ref[...] = _layer_norm(DN_ALPHA * x_ref[...] + c, vec_ref[0:1, :], vec_ref[1:2, :])


def _peer_sweep(x, xt, rk, e2, e1, nsel, u_bf, vt_bf, vec):
    n = x.shape[0]
    tb, eb = PEER_TOK_TILE, PEER_EXP_TILE
    tab = pl.BlockSpec((PEER_HEADS, PEER_NKEYS, tb), lambda i, e: (0, 0, i))
    return pl.pallas_call(
        _peer_sweep_kernel,
        grid=(n // tb, PEER_NEXPERTS // eb),
        in_specs=[pl.BlockSpec((tb, D_MODEL), lambda i, e: (i, 0)),
                  pl.BlockSpec((D_MODEL, tb), lambda i, e: (0, i)),
                  tab, tab, tab, tab,
                  pl.BlockSpec((eb, D_MODEL), lambda i, e: (e, 0)),
                  pl.BlockSpec((D_MODEL, eb), lambda i, e: (0, e)),
                  pl.BlockSpec(vec.shape, lambda i, e: (0, 0))],
        out_specs=pl.BlockSpec((tb, D_MODEL), lambda i, e: (i, 0)),
        out_shape=jax.ShapeDtypeStruct((n, D_MODEL), F32),
        scratch_shapes=[pltpu.VMEM((D_MODEL, tb), F32)],
        compiler_params=_cparams(2),
    )(x, xt, rk, e2, e1, nsel, u_bf, vt_bf, vec)


def _ple_kernel(x_ref, p_ref, wp_ref, wg_ref, bg_ref, *rest, n_extra):
    we_refs, o_ref, e_refs = rest[:n_extra], rest[n_extra], rest[n_extra + 1:]
    x = x_ref[...]
    gate = jax.nn.sigmoid(_dot(x.astype(BF16), wg_ref[...]) + bg_ref[...])
    x3 = x + _dot(p_ref[...].astype(BF16), wp_ref[...]) * gate
    o_ref[...] = x3
    x3b = x3.astype(BF16)
    for we_ref, e_ref in zip(we_refs, e_refs):
        e_ref[...] = _dot(x3b, we_ref[...])


def _ple(x, p, w_p, w_g, b_g, w_extra=()):
    n = x.shape[0]
    tm = ROW_TILE
    row = pl.BlockSpec((tm, D_MODEL), lambda i: (i, 0))
    n_extra = len(w_extra)
    return pl.pallas_call(
        functools.partial(_ple_kernel, n_extra=n_extra),
        grid=(n // tm,),
        in_specs=[row, pl.BlockSpec((tm, p.shape[1]), lambda i: (i, 0)),
                  _full(w_p.shape), _full(w_g.shape), _full(b_g.shape)]
                 + [_full(w.shape) for w in w_extra],
        out_specs=[row] * (1 + n_extra),
        out_shape=[jax.ShapeDtypeStruct((n, D_MODEL), F32)] * (1 + n_extra),
        compiler_params=_cparams(1),
    )(x, p, w_p, w_g, b_g, *w_extra)


def _subln(of, g, lam_init):
    of = of * lax.rsqrt(jnp.mean(of * of, axis=-1, keepdims=True) + RMS_EPS) * g
    return of * (1.0 - lam_init)


def _attn_prompt_kernel(scal_ref, q_ref, k_ref, v_ref, g_ref, o_ref, kb_ref, vt_ref, bias_ref,
                        diag_ref, m_ref, l_ref, acc_ref, *, lam_init):
    h = pl.program_id(1)
    qi = pl.program_id(2)
    tq = ATT_TILE
    slope = scal_ref[h]
    lam = scal_ref[DA_HEADS]

    @pl.when(qi == 0)
    def _():
        kb_ref[...] = k_ref[...].astype(BF16)
        vt_ref[...] = v_ref[...].T.astype(BF16)
        d0 = (lax.broadcasted_iota(jnp.int32, (tq, 2 * tq), 1) % tq
              - lax.broadcasted_iota(jnp.int32, (tq, 2 * tq), 0))
        bias = slope * d0.astype(F32)
        bias_ref[...] = bias
        diag_ref[...] = jnp.where(d0 >= 0, bias, -NEG_INF)

    qt = (q_ref[...] * (DA_HEAD ** -0.5)).T
    sub = lax.broadcasted_iota(jnp.int32, qt.shape, 0)
    qs = jnp.concatenate([jnp.where(sub < DA_HEAD, qt, 0.0), jnp.where(sub >= DA_HEAD, qt, 0.0)],
                         axis=1).astype(BF16)
    m_ref[...] = jnp.full_like(m_ref, -jnp.inf)
    l_ref[...] = jnp.zeros_like(l_ref)
    acc_ref[...] = jnp.zeros_like(acc_ref)

    def block(ki, bias, shift):
        off = pl.multiple_of(ki * tq, tq)
        s = _dot(kb_ref[pl.ds(off, tq), :], qs) - bias
        m_prev = m_ref[...]
        m_new = jnp.maximum(m_prev, jnp.max(s, axis=0, keepdims=True) - shift)
        alpha = jnp.exp(m_prev - m_new)
        p = jnp.exp(s - (m_new + shift))
        l_ref[...] = alpha * l_ref[...] + jnp.sum(p, axis=0, keepdims=True)
        acc_ref[...] = alpha * acc_ref[...] + _dot(vt_ref[:, pl.ds(off, tq)], p.astype(BF16))
        m_ref[...] = m_new

    def past(ki, carry):
        block(ki, bias_ref[...], slope * ((qi - ki) * tq).astype(F32))
        return carry

    lax.fori_loop(0, qi, past, 0)
    block(qi, diag_ref[...], 0.0)
    o = acc_ref[...] / l_ref[...]
    of = (o[:, :tq] - lam * o[:, tq:]).T
    o_ref[...] = _subln(of, g_ref[...], lam_init)


def _attn_prompt(scal, q, k, v, g, batch, seq, lam_init):
    tq = ATT_TILE
    nq = seq // tq
    grid_spec = pltpu.PrefetchScalarGridSpec(
        num_scalar_prefetch=1,
        grid=(batch, DA_HEADS, nq),
        in_specs=[pl.BlockSpec((tq, DA_VDIM), lambda b, h, i, s: (b * nq + i, h)),
                  pl.BlockSpec((seq, DA_VDIM), lambda b, h, i, s: (b, h)),
                  pl.BlockSpec((seq, DA_VDIM), lambda b, h, i, s: (b, h)),
                  pl.BlockSpec((1, DA_VDIM), lambda b, h, i, s: (0, 0))],
        out_specs=pl.BlockSpec((tq, DA_VDIM), lambda b, h, i, s: (b * nq + i, h)),
        scratch_shapes=[pltpu.VMEM((seq, DA_VDIM), BF16), pltpu.VMEM((DA_VDIM, seq), BF16),
                        pltpu.VMEM((tq, 2 * tq), F32), pltpu.VMEM((tq, 2 * tq), F32),
                        pltpu.VMEM((1, 2 * tq), F32), pltpu.VMEM((1, 2 * tq), F32),
                        pltpu.VMEM((DA_VDIM, 2 * tq), F32)])
    return pl.pallas_call(
        functools.partial(_attn_prompt_kernel, lam_init=lam_init),
        grid_spec=grid_spec,
        out_shape=jax.ShapeDtypeStruct((batch * seq, D_MODEL), F32),
        compiler_params=_cparams(3),
    )(scal, q, k, v, g)


NEW_TOKENS_PAD = LANES // DA_HEADS


def _attn_sample_kernel(pt_ref, scal_ref, q_ref, *rest, dec_seq, page, lam_init):
    k_refs, v_refs = rest[:SAMPLE_PAGES], rest[SAMPLE_PAGES:2 * SAMPLE_PAGES]
    (kn_ref, vn_ref, bias_ref, biasn_ref, g_ref, o_ref, qs_ref, m_ref, l_ref, acc_ref
     ) = rest[2 * SAMPLE_PAGES:]
    step = pl.program_id(1)
    rows = 2 * dec_seq * DA_HEADS
    row_h = lax.broadcasted_iota(jnp.int32, (rows, 1), 0) % DA_HEADS
    slope = jnp.zeros((rows, 1), F32)
    for h in range(DA_HEADS):
        slope = jnp.where(row_h == h, scal_ref[h], slope)

    @pl.when(step == 0)
    def _():
        q = q_ref[0] * (DA_HEAD ** -0.5)
        lane = lax.broadcasted_iota(jnp.int32, q.shape, 1)
        qs_ref[...] = jnp.concatenate(
            [jnp.where(lane < DA_HEAD, q, 0.0), jnp.where(lane >= DA_HEAD, q, 0.0)],
            axis=0).astype(BF16)
        m_ref[...] = jnp.full_like(m_ref, -jnp.inf)
        l_ref[...] = jnp.zeros_like(l_ref)
        acc_ref[...] = jnp.zeros_like(acc_ref)

    def update(scores, values):
        m_prev = m_ref[...]
        m_new = functools.reduce(jnp.maximum,
                                 [jnp.max(s, axis=-1, keepdims=True) for s in scores], m_prev)
        alpha = jnp.exp(m_prev - m_new)
        l_new = alpha * l_ref[...]
        acc = alpha * acc_ref[...]
        for s, vb in zip(scores, values):
            pr = jnp.exp(s - m_new)
            l_new = l_new + jnp.sum(pr, axis=-1, keepdims=True)
            acc = acc + _dot(pr.astype(BF16), vb)
        m_ref[...] = m_new
        l_ref[...] = l_new
        acc_ref[...] = acc

    flat = lambda ref: ref[0].reshape(page * DA_HEADS, DA_VDIM).astype(BF16)
    scores = []
    for j in range(SAMPLE_PAGES):
        first_tok = ((step * SAMPLE_PAGES + j) * page).astype(F32)
        scores.append(_dot_nt(qs_ref[...], flat(k_refs[j])) - bias_ref[...] + slope * first_tok)
    update(scores, [flat(v_ref) for v_ref in v_refs])

    @pl.when(step == pl.num_programs(1) - 1)
    def _():
        update([_dot_nt(qs_ref[...], kn_ref[0].astype(BF16)) - biasn_ref[...]],
               [vn_ref[0].astype(BF16)])
        o = acc_ref[...] / l_ref[...]
        of = o[:rows // 2] - scal_ref[DA_HEADS] * o[rows // 2:]
        o_ref[0] = _subln(of, g_ref[...], lam_init)


def _attn_sample(page_table, scal, q, cache_k, cache_v, k_new, v_new, g, lam_init):
    b = q.shape[0]
    dec_seq = q.shape[1] // DA_HEADS
    n_pages = page_table.shape[1]
    page = cache_k.shape[1]
    past_len = n_pages * page
    rows = 2 * dec_seq * DA_HEADS
    assert n_pages % SAMPLE_PAGES == 0 and dec_seq <= NEW_TOKENS_PAD

    slopes = scal[:DA_HEADS]
    r = jnp.arange(rows)
    r_q, r_h = (r // DA_HEADS) % dec_seq, r % DA_HEADS

    def bias_tile(n_tok, first_q_pos, causal):
        col = jnp.arange(n_tok * DA_HEADS)
        dist = first_q_pos + r_q[:, None] - (col // DA_HEADS)[None, :]
        ok = (col % DA_HEADS)[None, :] == r_h[:, None]
        if causal:
            ok = ok & (dist >= 0)
        return jnp.where(ok, slopes[r_h][:, None] * dist.astype(F32), -NEG_INF)

    bias_past = bias_tile(page, past_len, False)
    bias_new = bias_tile(NEW_TOKENS_PAD, 0, True)

    def page_spec(j):
        return pl.BlockSpec((1, page, DA_HEADS, DA_VDIM),
                            lambda i, p, pt, s: (pt[i * n_pages + p * SAMPLE_PAGES + j], 0, 0, 0))

    per_seq = lambda shape: pl.BlockSpec((1,) + shape, lambda i, p, pt, s: (i, 0, 0))
    const = lambda shape: pl.BlockSpec(shape, lambda i, p, pt, s: (0, 0))
    grid_spec = pltpu.PrefetchScalarGridSpec(
        num_scalar_prefetch=2,
        grid=(b, n_pages // SAMPLE_PAGES),
        in_specs=[per_seq((dec_seq * DA_HEADS, DA_VDIM))]
                 + [page_spec(j) for j in range(SAMPLE_PAGES)] * 2
                 + [per_seq((LANES, DA_VDIM)), per_seq((LANES, DA_VDIM)),
                    const(bias_past.shape), const(bias_new.shape), const((1, DA_VDIM))],
        out_specs=per_seq((dec_seq * DA_HEADS, DA_VDIM)),
        scratch_shapes=[pltpu.VMEM((rows, DA_VDIM), BF16),
                        pltpu.VMEM((rows, 1), F32), pltpu.VMEM((rows, 1), F32),
                        pltpu.VMEM((rows, DA_VDIM), F32)])
    return pl.pallas_call(
        functools.partial(_attn_sample_kernel, dec_seq=dec_seq, page=page, lam_init=lam_init),
        grid_spec=grid_spec,
        out_shape=jax.ShapeDtypeStruct(q.shape, F32),
        compiler_params=_cparams(2),
    )(page_table.reshape(-1), scal, q, *([cache_k] * SAMPLE_PAGES), *([cache_v] * SAMPLE_PAGES),
      k_new, v_new, bias_past, bias_new, g)


def _lambda_init(layer):
    return 0.8 - 0.6 * math.exp(-0.3 * layer)


def _peer_weights(layer, peer_w_q, peer_b_q, peer_subkeys, peer_u, peer_v, ln2_g, ln2_b):
    wq_t = peer_w_q[layer].T.astype(BF16)
    bq_b = jnp.broadcast_to(peer_b_q[layer][:, None], (PEER_HEADS * PEER_QDIM, ROUTE_TILE))
    return (wq_t, bq_b, peer_subkeys[layer].astype(BF16), peer_u[layer].astype(BF16),
            peer_v[layer].astype(BF16).T, jnp.stack([ln2_g[layer], ln2_b[layer]]))


def _peer_layer(x, wq_t, bq_b, subkeys, u_bf, vt_bf, ln_vec):
    xt, rk, e2, e1, nsel = _peer_route(x, wq_t, bq_b, subkeys)
    return _peer_sweep(x, xt, rk, e2, e1, nsel, u_bf, vt_bf, ln_vec)


def kernel(x_prompt, x_sample, state_shift, state_wkv, cache_k, cache_v, page_table, p_prompt, p_sample, rw_mu, rw_w_r, rw_w_k, rw_w_v, rw_w0, rw_w1, rw_w2, rw_a0, rw_a1, rw_a2, rw_g1, rw_g2, rw_k_k, rw_k_a, rw_r_k, rw_gn_g, rw_gn_b, rw_w_o, da_w_k, da_w_v, da_w_q, da_lam, da_subln_g, da_w_o, ln1_g, ln1_b, ln2_g, ln2_b, peer_w_q, peer_b_q, peer_subkeys, peer_u, peer_v, ple_w_p, ple_w_g, ple_b_g):
    bp, tp, d = x_prompt.shape
    bs, ts, _ = x_sample.shape
    assert d == D_MODEL and bp * RW_HEADS == LANES and bs == LANES
    np_, ns = bp * tp, bs * ts
    hh, hd = RW_HEADS, RW_HEAD
    bf = lambda w: w.astype(BF16)

    head_of_lane = jnp.arange(d) // hd
    ind = (head_of_lane[:, None] == jnp.arange(LANES)[None, :]).astype(BF16)
    ind_t = ind.T
    pre_consts = (rw_mu[0], bf(rw_w_r[0]), bf(rw_w_k[0]), bf(rw_w_v[0]), bf(rw_w1[0]), bf(rw_w2[0]),
                  bf(rw_a1[0]), bf(rw_a2[0]), bf(rw_g1[0]), bf(rw_g2[0]),
                  jnp.stack([rw_w0[0], rw_a0[0], rw_k_k[0], rw_k_a[0]]), ind, ind_t)
    post_consts = (jnp.stack([rw_gn_g[0], rw_gn_b[0], rw_r_k[0].reshape(d), ln1_g[0], ln1_b[0]]),
                   ind, ind_t, bf(rw_w_o[0]))
    peer_consts = [_peer_weights(layer, peer_w_q, peer_b_q, peer_subkeys, peer_u, peer_v, ln2_g, ln2_b)
                   for layer in range(DEPTH)]
    ple_consts = [(bf(ple_w_p[layer]), bf(ple_w_g[layer]), ple_b_g[layer][None])
                  for layer in range(DEPTH)]
    kvq_w = (bf(da_w_k), bf(da_w_v), bf(da_w_q[0]))
    attn_out = (jnp.stack([ln1_g[1], ln1_b[1]]), bf(da_w_o[0]))
    lam_init = _lambda_init(1)
    lv = da_lam[0].astype(F32)
    lam = jnp.exp(jnp.sum(lv[0] * lv[1])) - jnp.exp(jnp.sum(lv[2] * lv[3])) + lam_init
    slopes = 2.0 ** (-8.0 * jnp.arange(1, DA_HEADS + 1, dtype=F32) / DA_HEADS)
    scal = jnp.concatenate([slopes, lam[None]]).astype(F32)
    subln_g = da_subln_g[0][None]

    def trunk(mixer, p, attend):
        x, state = mixer()
        x = _peer_layer(x, *peer_consts[0])
        x, k_sh, v_sh, q = _ple(x, p[0], *ple_consts[0], kvq_w)
        x = _proj_ln(x, attend(q, k_sh, v_sh), *attn_out)
        x = _peer_layer(x, *peer_consts[1])
        (x,) = _ple(x, p[1], *ple_consts[1])
        return x, state, k_sh, v_sh

    def prompt_mixer():
        r, w, k, v, a, b, g = _rwkv_pre_tm(x_prompt, jnp.zeros((bp, d), F32), *pre_consts)
        y, state = _wkv_scan_tm(r, w, k, v, a, b, bp)
        return _rwkv_post_tm(x_prompt, y, r, k, v, g, *post_consts).reshape(np_, d), state

    y_prompt, st_p, k_p, v_p = trunk(
        prompt_mixer, p_prompt.reshape(DEPTH, np_, -1),
        lambda q, k_sh, v_sh: _attn_prompt(scal, q, k_sh, v_sh, subln_g, bp, tp, lam_init))

    def sample_mixer():
        x = x_sample.reshape(ns, d)
        x_prev = jnp.concatenate([state_shift[0][:, None, :], x_sample[:, :-1]], axis=1).reshape(ns, d)
        r, w, k, v, a, b, g = _rwkv_pre(x, x_prev, *pre_consts)
        to_chains = lambda t: t.reshape(bs, ts, hh, hd).transpose(2, 1, 3, 0)
        y, state = _wkv_scan(*(to_chains(t) for t in (r, w, k, v, a, b)),
                             state_wkv[0].astype(F32).transpose(1, 2, 3, 0))
        y = y.transpose(3, 1, 0, 2).reshape(ns, d)
        return _rwkv_post(x, y, r, k, v, g, *post_consts), state

    new_rows = lambda t: jnp.pad(t.reshape(bs, ts, d), ((0, 0), (0, NEW_TOKENS_PAD - ts), (0, 0))
                                 ).reshape(bs, LANES, DA_VDIM)
    y_sample, st_s, k_s, v_s = trunk(
        sample_mixer, p_sample.reshape(DEPTH, ns, -1),
        lambda q, k_sh, v_sh: _attn_sample(
            page_table, scal, q.reshape(bs, ts * DA_HEADS, DA_VDIM), cache_k, cache_v,
            new_rows(k_sh), new_rows(v_sh), subln_g, lam_init).reshape(ns, d))

    wkv_p = st_p.reshape(hd, hd, 2, hh // 2, bp).transpose(4, 3, 2, 0, 1).reshape(bp, hh, hd, hd)
    wkv_p = wkv_p[None].astype(state_wkv.dtype)
    wkv_s = st_s.transpose(3, 0, 1, 2)[None].astype(state_wkv.dtype)
    kv_shape_p = (bp, tp, DA_HEADS, DA_VDIM)
    kv_shape_s = (bs, ts, DA_HEADS, DA_VDIM)
    return (y_prompt.reshape(bp, tp, d), y_sample.reshape(bs, ts, d),
            x_prompt[:, -1][None], wkv_p, k_p.reshape(kv_shape_p), v_p.reshape(kv_shape_p),
            x_sample[:, -1][None], wkv_s, k_s.reshape(kv_shape_s), v_s.reshape(kv_shape_s))
```
